```python
import jax, jax.numpy as jnp
from jax import lax
import numpy as np

D_MODEL = 1024
BATCH = 2
SEQ = 8192
DEPTH = 4

GLA_HEADS = 4
GLA_DK = 128
GLA_DV = 128
GLA_QK = GLA_HEADS * GLA_DK
GLA_V = GLA_HEADS * GLA_DV
GLA_GATE_RANK = 16
GLA_GATE_NORM = 16.0
GLA_CHUNK = 64
GMLP_GROUPS = 4
GMLP_GROUP_CH = 128
GMLP_WIDTH = GMLP_GROUPS * GMLP_GROUP_CH
GMLP_CHUNK = 128
MOBA_HEADS = 4
MOBA_HD = 128
MOBA_W = MOBA_HEADS * MOBA_HD
MOBA_BLOCK = 256
MOBA_TOPK = 3
MOBA_QBLOCK = 64
ROPE_THETA = 500000.0
ROPE_DIMS = MOBA_HD // 4
N_BRANCH = 3
BRANCH_W = 512
D_FF = 4 * D_MODEL
IN_COLS = 2 * GLA_QK + 2 * GLA_V + GLA_GATE_RANK + 2 * GMLP_WIDTH + 3 * MOBA_W + N_BRANCH * D_MODEL
DEEPNORM_ALPHA = (2 * DEPTH) ** 0.25
DEEPNORM_BETA = (8 * DEPTH) ** -0.25
LN_EPS = 1e-5
RMS_EPS = 1e-6

kernel_name = 'hybrid_gla_gmlp_moba_deepnorm'


def layer_norm(x, g, b):
    xf = x.astype(jnp.float32)
    mu = jnp.mean(xf, axis=-1, keepdims=True)
    xc = xf - mu
    var = jnp.mean(xc * xc, axis=-1, keepdims=True)
    return (xc * lax.rsqrt(var + LN_EPS) * g + b).astype(x.dtype)


def partial_rope(t, positions):
    half = ROPE_DIMS // 2
    inv = 1.0 / (ROPE_THETA ** (jnp.arange(half, dtype=jnp.float32) * (2.0 / ROPE_DIMS)))
    ang = positions.astype(jnp.float32)[:, :, None] * inv
    cos = jnp.cos(ang)[:, :, None, :]
    sin = jnp.sin(ang)[:, :, None, :]
    tr = t[..., :ROPE_DIMS].astype(jnp.float32)
    x1, x2 = tr[..., :half], tr[..., half:]
    rot = jnp.concatenate([x1 * cos - x2 * sin, x2 * cos + x1 * sin], axis=-1)
    return jnp.concatenate([rot.astype(t.dtype), t[..., ROPE_DIMS:]], axis=-1)


def gla_branch(q, k, v, g, lr, w_gate_up, b_gate, norm_w):
    B, S, _ = q.shape
    H, DK, DV, C = GLA_HEADS, GLA_DK, GLA_DV, GLA_CHUNK
    nC = S // C
    f32 = jnp.float32

    def heads(t, d):
        return t.astype(f32).reshape(B, nC, C, H, d).transpose(0, 3, 1, 2, 4)

    log_a = jax.nn.log_sigmoid((lr @ w_gate_up + b_gate).astype(f32)) / GLA_GATE_NORM
    bcum = jnp.cumsum(heads(log_a, DK), axis=3)
    qh = heads(q, DK) * (DK ** -0.5)
    kh = heads(k, DK)
    vh = heads(v, DV)
    q_dec = qh * jnp.exp(bcum)
    k_inv = kh * jnp.exp(-bcum)
    causal = jnp.tril(jnp.ones((C, C), dtype=bool))
    attn = jnp.where(causal, jnp.einsum('bhntd,bhnsd->bhnts', q_dec, k_inv), 0.0)
    o_intra = jnp.einsum('bhnts,bhnsv->bhntv', attn, vh)
    b_end = bcum[:, :, :, -1:, :]
    kv_chunk = jnp.einsum('bhnsd,bhnsv->bhndv', kh * jnp.exp(b_end - bcum), vh)
    decay = jnp.exp(b_end[:, :, :, 0, :])

    def step(state, inp):
        kv_n, dec_n = inp
        return dec_n[..., None] * state + kv_n, state

    _, states = lax.scan(step, jnp.zeros((B, H, DK, DV), f32),
                         (jnp.moveaxis(kv_chunk, 2, 0), jnp.moveaxis(decay, 2, 0)))
    states = jnp.moveaxis(states, 0, 2)
    o = o_intra + jnp.einsum('bhntd,bhndv->bhntv', q_dec, states)
    o = o.transpose(0, 2, 3, 1, 4).reshape(B, S, H, DV)
    o = o * lax.rsqrt(jnp.mean(o * o, axis=-1, keepdims=True) + RMS_EPS) * norm_w
    o = o.reshape(B, S, H * DV) * jax.nn.silu(g.astype(f32))
    return o.astype(q.dtype)


def gmlp_branch(z, ln_g, ln_b, w_s, b_s):
    B, S, _ = z.shape
    G, CH, C = GMLP_GROUPS, GMLP_GROUP_CH, GMLP_CHUNK
    z = jax.nn.gelu(z, approximate=False)
    u, v = jnp.split(z, 2, axis=-1)
    v = layer_norm(v, ln_g, ln_b).reshape(B, S // C, C, G, CH)
    w = w_s * jnp.tril(jnp.ones((C, C), dtype=w_s.dtype))
    vs = jnp.einsum('gts,bnsgc->bntgc', w, v) + b_s.T[None, None, :, :, None]
    return u * vs.reshape(B, S, GMLP_WIDTH)


def moba_branch(q, k, v, positions):
    B, S, _ = q.shape
    H, HD, BLK, QB = MOBA_HEADS, MOBA_HD, MOBA_BLOCK, MOBA_QBLOCK
    f32 = jnp.float32
    qh = partial_rope(q.reshape(B, S, H, HD), positions).transpose(0, 2, 1, 3)
    kh = partial_rope(k.reshape(B, S, H, HD), positions).transpose(0, 2, 1, 3)
    vh = v.reshape(B, S, H, HD).transpose(0, 2, 1, 3)
    n_blk = -(-S // BLK)
    s_pad = n_blk * BLK
    pad = ((0, 0), (0, 0), (0, s_pad - S), (0, 0))
    kh = jnp.pad(kh, pad)
    vh = jnp.pad(vh, pad)
    k_blocks = kh.reshape(B, H, n_blk, BLK, HD)
    v_blocks = vh.reshape(B, H, n_blk, BLK, HD)
    k_mean = jnp.mean(k_blocks.astype(f32), axis=3)
    own_blk = jnp.arange(S) // BLK
    fully_past = jnp.arange(n_blk)[None, :] < own_blk[:, None]
    gate = jnp.where(fully_past, jnp.einsum('bhsd,bhnd->bhsn', qh.astype(f32), k_mean), -jnp.inf)
    n_sel = min(MOBA_TOPK, n_blk)
    _, sel = lax.top_k(gate, n_sel)
    nQ = S // QB
    q_c = jnp.moveaxis(qh.reshape(B, H, nQ, QB, HD), 2, 0)
    sel_c = jnp.moveaxis(sel.reshape(B, H, nQ, QB, n_sel), 2, 0)
    starts = jnp.arange(nQ, dtype=jnp.int32) * QB
    gather = jax.vmap(jax.vmap(lambda blocks, ids: blocks[ids]))
    scale = HD ** -0.5

    def attend(args):
        qb, sb, start = args
        t = start + jnp.arange(QB)
        blk_start = (start // BLK) * BLK
        kg = gather(k_blocks, sb)
        vg = gather(v_blocks, sb)
        s_g = jnp.einsum('bhqd,bhqjkd->bhqjk', qb, kg).astype(f32) * scale
        valid = jnp.arange(n_sel)[None, :] < (t // BLK)[:, None]
        s_g = jnp.where(valid[:, :, None], s_g, -jnp.inf).reshape(B, H, QB, n_sel * BLK)
        k_own = lax.dynamic_slice_in_dim(kh, blk_start, BLK, axis=2)
        v_own = lax.dynamic_slice_in_dim(vh, blk_start, BLK, axis=2)
        s_o = jnp.einsum('bhqd,bhkd->bhqk', qb, k_own).astype(f32) * scale
        s_o = jnp.where((blk_start + jnp.arange(BLK))[None, :] <= t[:, None], s_o, -jnp.inf)
        p = jax.nn.softmax(jnp.concatenate([s_g, s_o], axis=-1), axis=-1).astype(vh.dtype)
        p_g = p[..., :n_sel * BLK].reshape(B, H, QB, n_sel, BLK)
        return (jnp.einsum('bhqjk,bhqjkd->bhqd', p_g, vg)
                + jnp.einsum('bhqk,bhkd->bhqd', p[..., n_sel * BLK:], v_own))

    out = lax.map(attend, (q_c, sel_c, starts))
    out = jnp.moveaxis(out, 0, 2).reshape(B, H, S, HD)
    return out.transpose(0, 2, 1, 3).reshape(B, S, MOBA_W)


def mixer_sublayer(x, positions, w_in, w_gate_up, b_gate, gla_norm_w, gmlp_ln_g, gmlp_ln_b,
                   gmlp_w_s, gmlp_b_s, w_branch, w_out):
    sizes = [GLA_QK, GLA_QK, GLA_V, GLA_V, GLA_GATE_RANK, 2 * GMLP_WIDTH,
             MOBA_W, MOBA_W, MOBA_W, N_BRANCH * D_MODEL]
    offsets = [int(o) for o in np.cumsum(sizes)[:-1]]
    proj = x @ w_in
    gq, gk, gv, gg, glr, gz, mq, mk, mv, gl = jnp.split(proj, offsets, axis=-1)
    a = gla_branch(gq, gk, gv, gg, glr, w_gate_up, b_gate, gla_norm_w) @ w_branch[0]
    b = gmlp_branch(gz, gmlp_ln_g, gmlp_ln_b, gmlp_w_s, gmlp_b_s) @ w_branch[1]
    c = moba_branch(mq, mk, mv, positions) @ w_branch[2]
    ga, gb, gc = jnp.split(jax.nn.sigmoid(gl), N_BRANCH, axis=-1)
    return (ga * a + gb * b + gc * c) @ w_out


def setup_inputs(seed: int = 0) -> dict:
    key = jax.random.key(seed)
    ks = jax.random.split(key, 20)
    L, D = DEPTH, D_MODEL
    nrm = lambda k, shape, s: jax.random.normal(k, shape, jnp.float32) * s
    offset = jax.random.randint(ks[1], (BATCH, 1), 0, 1024, dtype=jnp.int32)
    return {
        'x': nrm(ks[0], (BATCH, SEQ, D), 1.0),
        'positions': offset + jnp.arange(SEQ, dtype=jnp.int32)[None, :],
        'w_in': nrm(ks[2], (L, D, IN_COLS), D ** -0.5),
        'w_gate_up': nrm(ks[3], (L, GLA_GATE_RANK, GLA_QK), GLA_GATE_RANK ** -0.5),
        'b_gate': nrm(ks[4], (L, GLA_QK), 0.1),
        'gla_norm_w': 1.0 + nrm(ks[5], (L, GLA_DV), 0.02),
        'gmlp_ln_g': 1.0 + nrm(ks[6], (L, GMLP_WIDTH), 0.02),
        'gmlp_ln_b': nrm(ks[7], (L, GMLP_WIDTH), 0.02),
        'gmlp_w_s': nrm(ks[8], (L, GMLP_GROUPS, GMLP_CHUNK, GMLP_CHUNK), 0.5 * GMLP_CHUNK ** -0.5),
        'gmlp_b_s': 1.0 + nrm(ks[9], (L, GMLP_GROUPS, GMLP_CHUNK), 0.1),
        'w_branch': nrm(ks[10], (L, N_BRANCH, BRANCH_W, D), BRANCH_W ** -0.5),
        'w_out': nrm(ks[11], (L, D, D), DEEPNORM_BETA * D ** -0.5),
        'ln1_g': 1.0 + nrm(ks[12], (L, D), 0.02),
        'ln1_b': nrm(ks[13], (L, D), 0.02),
        'w_ff1': nrm(ks[14], (L, D, D_FF), D ** -0.5),
        'w_ff2': nrm(ks[15], (L, D_FF, D), DEEPNORM_BETA * D_FF ** -0.5),
        'ln2_g': 1.0 + nrm(ks[16], (L, D), 0.02),
        'ln2_b': nrm(ks[17], (L, D), 0.02),
    }


def reference(x, positions, w_in, w_gate_up, b_gate, gla_norm_w, gmlp_ln_g, gmlp_ln_b, gmlp_w_s,
              gmlp_b_s, w_branch, w_out, ln1_g, ln1_b, w_ff1, w_ff2, ln2_g, ln2_b):
    for l in range(DEPTH):
        mix = mixer_sublayer(x, positions, w_in[l], w_gate_up[l], b_gate[l], gla_norm_w[l],
                             gmlp_ln_g[l], gmlp_ln_b[l], gmlp_w_s[l], gmlp_b_s[l], w_branch[l], w_out[l])
        x = layer_norm(DEEPNORM_ALPHA * x + mix, ln1_g[l], ln1_b[l])
        ff = jnp.square(jax.nn.relu(x @ w_ff1[l])) @ w_ff2[l]
        x = layer_norm(DEEPNORM_ALPHA * x + ff, ln2_g[l], ln2_b[l])
    return x
```

```python
import functools

import numpy as np
import jax
import jax.numpy as jnp
from jax import lax
from jax.experimental import pallas as pl
from jax.experimental.pallas import tpu as pltpu

D_MODEL = 1024
DEPTH = 4
GLA_HEADS = 4
GLA_DK = 128
GLA_DV = 128
GLA_GATE_RANK = 16
GLA_GATE_NORM = 16.0
GLA_CHUNK = 64
GMLP_GROUPS = 4
GMLP_GROUP_CH = 128
GMLP_WIDTH = 512
GMLP_CHUNK = 128
MOBA_HEADS = 4
MOBA_HD = 128
MOBA_BLOCK = 256
MOBA_TOPK = 3
ROPE_THETA = 500000.0
ROPE_DIMS = MOBA_HD // 4
BRANCH_W = 512
D_FF = 4 * D_MODEL
DEEPNORM_ALPHA = (2 * DEPTH) ** 0.25
LN_EPS = 1e-5
RMS_EPS = 1e-6

LANES = 128
VMEM_LIMIT = 56 * 1024 * 1024

COL_GL = 0
COL_GQ = 3072
COL_GK = 3584
COL_GV = 4096
COL_GG = 4608
COL_GZ = 5120
COL_MQ = 6144
COL_MK = 6656
COL_MV = 7168
PROJ_COLS = 7680

BF16 = jnp.bfloat16
F32 = jnp.float32


def _dot(a, b):
    return jnp.dot(a, b, preferred_element_type=F32)


def _dot_nt(a, b):
    return lax.dot_general(a, b, (((1,), (1,)), ((), ())), preferred_element_type=F32)


def _dot_tn(a, b):
    return lax.dot_general(a, b, (((0,), (0,)), ((), ())), preferred_element_type=F32)


def _params(*sem):
    return pltpu.CompilerParams(dimension_semantics=sem, vmem_limit_bytes=VMEM_LIMIT)


def _layer_norm(y, g, b):
    mu = jnp.mean(y, axis=-1, keepdims=True)
    yc = y - mu
    var = jnp.mean(yc * yc, axis=-1, keepdims=True)
    return yc * lax.rsqrt(var + LN_EPS) * g + b


def _matmul_kernel(x_ref, w_ref, o_ref):
    o_ref[...] = _dot(x_ref[...], w_ref[...]).astype(o_ref.dtype)


def _matmul(x, w, *, tm, tn, out_dtype):
    m, k = x.shape
    n = w.shape[1]
    return pl.pallas_call(
        _matmul_kernel,
        grid=(m // tm, n // tn),
        in_specs=[pl.BlockSpec((tm, k), lambda i, j: (i, 0)),
                  pl.BlockSpec((k, tn), lambda i, j: (0, j))],
        out_specs=pl.BlockSpec((tm, tn), lambda i, j: (i, j)),
        out_shape=jax.ShapeDtypeStruct((m, n), out_dtype),
        compiler_params=_params("parallel", "parallel"),
        name="in_proj",
    )(x, w)


def _gla_kernel(x_ref, q_ref, k_ref, v_ref, g_ref, wlr_ref, wgu_ref, bg_ref, nw_ref,
                o_ref, st_ref, la_ref):
    rows = x_ref.shape[0]
    C = GLA_CHUNK

    @pl.when(pl.program_id(1) == 0)
    def _():
        st_ref[...] = jnp.zeros_like(st_ref)

    lr = _dot(x_ref[...], wlr_ref[...]).astype(BF16)
    z = _dot(lr, wgu_ref[...]) + bg_ref[...]
    la_ref[...] = (jnp.minimum(z, 0.0) - jnp.log1p(jnp.exp(-jnp.abs(z)))) / GLA_GATE_NORM

    row = lax.broadcasted_iota(jnp.int32, (C, C), 0)
    col = lax.broadcasted_iota(jnp.int32, (C, C), 1)
    causal = row >= col
    tril = causal.astype(BF16)
    nw = nw_ref[...]

    def chunk(c, carry):
        r0 = pl.multiple_of(c * C, C)
        la = la_ref[pl.ds(r0, C), :]
        hi = la.astype(BF16)
        r1 = la - hi.astype(F32)
        mid = r1.astype(BF16)
        lo = (r1 - mid.astype(F32)).astype(BF16)
        bcum = _dot(tril, hi) + _dot(tril, mid) + _dot(tril, lo)
        b_end = bcum[C - 1:C, :]
        q = q_ref[pl.ds(r0, C), :].astype(F32) * (GLA_DK ** -0.5)
        k = k_ref[pl.ds(r0, C), :].astype(F32)
        v = v_ref[pl.ds(r0, C), :].astype(BF16)
        g = g_ref[pl.ds(r0, C), :].astype(F32)
        q_dec = (q * jnp.exp(bcum)).astype(BF16)
        k_inv = (k * jnp.exp(-bcum)).astype(BF16)
        k_end = (k * jnp.exp(b_end - bcum)).astype(BF16)
        decay = jnp.exp(b_end)
        for h in range(GLA_HEADS):
            sl = slice(h * LANES, (h + 1) * LANES)
            attn = jnp.where(causal, _dot_nt(q_dec[:, sl], k_inv[:, sl]), 0.0).astype(BF16)
            st = st_ref[h]
            o = _dot(attn, v[:, sl]) + _dot_nt(q_dec[:, sl], st.astype(BF16))
            st_ref[h] = decay[:, sl] * st + _dot_tn(v[:, sl], k_end[:, sl])
            o = o * lax.rsqrt(jnp.mean(o * o, axis=-1, keepdims=True) + RMS_EPS) * nw
            gh = g[:, sl]
            o = o * (gh * jax.nn.sigmoid(gh))
            o_ref[pl.ds(r0, C), sl] = o.astype(o_ref.dtype)
        return carry

    lax.fori_loop(0, rows // C, chunk, 0)


def _gla(xb, proj, w_lr, w_gu, b_gate, norm_w, *, batch, rows):
    t = xb.shape[0]
    steps = t // batch // rows
    w = GLA_HEADS * GLA_DK

    def col(off):
        return pl.BlockSpec((rows, w), lambda b, s: (b * steps + s, off // w))

    def full(shape):
        return pl.BlockSpec(shape, lambda b, s: (0,) * len(shape))

    return pl.pallas_call(
        _gla_kernel,
        grid=(batch, steps),
        in_specs=[pl.BlockSpec((rows, D_MODEL), lambda b, s: (b * steps + s, 0)),
                  col(COL_GQ), col(COL_GK), col(COL_GV), col(COL_GG),
                  full((D_MODEL, LANES)), full((LANES, w)), full((1, w)), full((1, GLA_DV))],
        out_specs=pl.BlockSpec((rows, w), lambda b, s: (b * steps + s, 0)),
        out_shape=jax.ShapeDtypeStruct((t, w), BF16),
        scratch_shapes=[pltpu.VMEM((GLA_HEADS, GLA_DV, GLA_DK), F32),
                        pltpu.VMEM((rows, w), F32)],
        compiler_params=_params("arbitrary", "arbitrary"),
        name="gla",
    )(xb, proj, proj, proj, proj, w_lr, w_gu, b_gate, norm_w)


def _gmlp_kernel(z_ref, lng_ref, lnb_ref, ws_ref, bs_ref, o_ref):
    rows = z_ref.shape[0]
    C = GMLP_CHUNK
    z = z_ref[...].astype(F32)
    z = 0.5 * z * (1.0 + lax.erf(z * np.float32(np.sqrt(0.5))))
    u = z[:, :GMLP_WIDTH]
    v = _layer_norm(z[:, GMLP_WIDTH:], lng_ref[...], lnb_ref[...]).astype(BF16)
    row = lax.broadcasted_iota(jnp.int32, (C, C), 0)
    col = lax.broadcasted_iota(jnp.int32, (C, C), 1)
    tril = (row >= col).astype(F32)
    for g in range(GMLP_GROUPS):
        sl = slice(g * GMLP_GROUP_CH, (g + 1) * GMLP_GROUP_CH)
        w = (ws_ref[g] * tril).astype(BF16)
        for c in range(rows // C):
            rs = slice(c * C, (c + 1) * C)
            vs = _dot(w, v[rs, sl]) + bs_ref[:, sl]
            o_ref[rs, sl] = (u[rs, sl] * vs).astype(o_ref.dtype)


def _gmlp(proj, ln_g, ln_b, w_s, bs_full, *, rows):
    t = proj.shape[0]

    def full(shape):
        return pl.BlockSpec(shape, lambda i: (0,) * len(shape))

    return pl.pallas_call(
        _gmlp_kernel,
        grid=(t // rows,),
        in_specs=[pl.BlockSpec((rows, 2 * GMLP_WIDTH), lambda i: (i, COL_GZ // (2 * GMLP_WIDTH))),
                  full((1, GMLP_WIDTH)), full((1, GMLP_WIDTH)),
                  full((GMLP_GROUPS, GMLP_CHUNK, GMLP_CHUNK)), full((GMLP_CHUNK, GMLP_WIDTH))],
        out_specs=pl.BlockSpec((rows, GMLP_WIDTH), lambda i: (i, 0)),
        out_shape=jax.ShapeDtypeStruct((t, GMLP_WIDTH), BF16),
        compiler_params=_params("parallel"),
        name="gmlp",
    )(proj, ln_g, ln_b, w_s, bs_full)


def _rope_table_kernel(pos_ref, inv_ref, c_ref, s1_ref, s2_ref):
    half = ROPE_DIMS // 2
    ang = pos_ref[...].astype(F32) * inv_ref[...]
    lane = lax.broadcasted_iota(jnp.int32, ang.shape, 1)
    cos = jnp.cos(ang)
    sin = jnp.sin(ang)
    c_ref[...] = jnp.where(lane < ROPE_DIMS, cos, 1.0)
    s1_ref[...] = jnp.where(lane < half, -sin, 0.0)
    s2_ref[...] = jnp.where((lane >= half) & (lane < ROPE_DIMS), sin, 0.0)


def _rope_tables(pos_col, inv_row, *, rows):
    t = pos_col.shape[0]
    spec = pl.BlockSpec((rows, LANES), lambda i: (i, 0))
    shp = jax.ShapeDtypeStruct((t, LANES), F32)
    return pl.pallas_call(
        _rope_table_kernel,
        grid=(t // rows,),
        in_specs=[pl.BlockSpec((rows, 1), lambda i: (i, 0)),
                  pl.BlockSpec((1, LANES), lambda i: (0, 0))],
        out_specs=[spec, spec, spec],
        out_shape=[shp, shp, shp],
        compiler_params=_params("parallel"),
        name="rope_tables",
    )(pos_col, inv_row)


def _moba_prep_kernel(q_ref, k_ref, c_ref, s1_ref, s2_ref, qo_ref, ko_ref, km_ref):
    half = ROPE_DIMS // 2
    cos = c_ref[...]
    s1 = s1_ref[...]
    s2 = s2_ref[...]

    def rope(t):
        return (t * cos + pltpu.roll(t, LANES - half, 1) * s1 + pltpu.roll(t, half, 1) * s2)

    for h in range(MOBA_HEADS):
        sl = slice(h * LANES, (h + 1) * LANES)
        qo_ref[:, sl] = rope(q_ref[:, sl].astype(F32)).astype(qo_ref.dtype)
        kr = rope(k_ref[:, sl].astype(F32))
        ko_ref[:, sl] = kr.astype(ko_ref.dtype)
        km_ref[0, :, sl] = jnp.mean(kr, axis=0, keepdims=True)


def _moba_prep(proj, cos, s1, s2):
    t = proj.shape[0]
    w = MOBA_HEADS * MOBA_HD
    nb = t // MOBA_BLOCK
    tab = pl.BlockSpec((MOBA_BLOCK, LANES), lambda i: (i, 0))
    rw = pl.BlockSpec((MOBA_BLOCK, w), lambda i: (i, 0))
    return pl.pallas_call(
        _moba_prep_kernel,
        grid=(nb,),
        in_specs=[pl.BlockSpec((MOBA_BLOCK, w), lambda i: (i, COL_MQ // w)),
                  pl.BlockSpec((MOBA_BLOCK, w), lambda i: (i, COL_MK // w)),
                  tab, tab, tab],
        out_specs=[rw, rw, pl.BlockSpec((1, 1, w), lambda i: (i, 0, 0))],
        out_shape=[jax.ShapeDtypeStruct((t, w), BF16), jax.ShapeDtypeStruct((t, w), BF16),
                   jax.ShapeDtypeStruct((nb, 1, w), F32)],
        compiler_params=_params("parallel"),
        name="moba_prep",
    )(proj, proj, cos, s1, s2)


def _moba_flash_kernel(q_ref, k_ref, v_ref, km_ref, o_ref):
    BLK = MOBA_BLOCK
    qi = pl.program_id(2)
    q = q_ref[...]
    scale = MOBA_HD ** -0.5
    n_blk = km_ref.shape[0]

    km = km_ref[...]
    km_hi = km.astype(BF16)
    km_lo = (km - km_hi.astype(F32)).astype(BF16)
    gate = _dot_nt(q, km_hi) + _dot_nt(q, km_lo)
    lane = lax.broadcasted_iota(jnp.int32, gate.shape, 1)
    lane_f = lane.astype(F32)
    cand = lane < qi
    picks = []
    for r in range(min(MOBA_TOPK, n_blk)):
        g_eff = jnp.where(cand, gate, -jnp.inf)
        best = jnp.max(g_eff, axis=-1, keepdims=True)
        idx = jnp.min(jnp.where(cand & (g_eff == best), lane_f, float(n_blk)),
                      axis=-1, keepdims=True)
        idx = jnp.where(r < qi, idx, float(n_blk))
        picks.append(idx)
        cand = cand & (lane_f != idx)

    r0 = pl.multiple_of(qi * BLK, BLK)
    row = lax.broadcasted_iota(jnp.int32, (BLK, BLK), 0)
    col = lax.broadcasted_iota(jnp.int32, (BLK, BLK), 1)
    s = _dot_nt(q, k_ref[pl.ds(r0, BLK), :]) * scale
    s = jnp.where(col <= row, s, -jnp.inf)
    m0 = jnp.max(s, axis=-1, keepdims=True)
    p = jnp.exp(s - m0)
    l0 = jnp.sum(p, axis=-1, keepdims=True)
    acc0 = _dot(p.astype(BF16), v_ref[pl.ds(r0, BLK), :].astype(BF16))

    def body(j, carry):
        m, l, acc = carry
        c0 = pl.multiple_of(j * BLK, BLK)
        jf = j.astype(F32)
        hit = picks[0] == jf
        for idx in picks[1:]:
            hit = hit | (idx == jf)
        s = _dot_nt(q, k_ref[pl.ds(c0, BLK), :]) * scale
        s = jnp.where(hit, s, -jnp.inf)
        m_new = jnp.maximum(m, jnp.max(s, axis=-1, keepdims=True))
        alpha = jnp.exp(m - m_new)
        p = jnp.exp(s - m_new)
        l = alpha * l + jnp.sum(p, axis=-1, keepdims=True)
        acc = alpha * acc + _dot(p.astype(BF16), v_ref[pl.ds(c0, BLK), :].astype(BF16))
        return m_new, l, acc

    m, l, acc = lax.fori_loop(0, qi, body, (m0, l0, acc0))
    o_ref[...] = (acc / l).astype(o_ref.dtype)


def _moba_flash(q_rot, k_rot, proj3, k_mean):
    batch, seq, w = q_rot.shape
    n_blk = seq // MOBA_BLOCK
    return pl.pallas_call(
        _moba_flash_kernel,
        grid=(batch, MOBA_HEADS, n_blk),
        in_specs=[pl.BlockSpec((None, MOBA_BLOCK, MOBA_HD), lambda b, h, i: (b, i, h)),
                  pl.BlockSpec((None, seq, MOBA_HD), lambda b, h, i: (b, 0, h)),
                  pl.BlockSpec((None, seq, MOBA_HD), lambda b, h, i: (b, 0, COL_MV // MOBA_HD + h)),
                  pl.BlockSpec((None, n_blk, MOBA_HD), lambda b, h, i: (b, 0, h))],
        out_specs=pl.BlockSpec((None, MOBA_BLOCK, MOBA_HD), lambda b, h, i: (b, i, h)),
        out_shape=jax.ShapeDtypeStruct((batch, seq, w), BF16),
        compiler_params=_params("parallel", "parallel", "arbitrary"),
        name="moba_flash",
    )(q_rot, k_rot, proj3, k_mean)


def _merge_kernel(a_ref, b_ref, c_ref, gl_ref, x_ref, wb_ref, wo_ref, g_ref, bt_ref,
                  xo_ref, xb_ref):
    gl = gl_ref[...].astype(F32)
    d = D_MODEL
    m = (jax.nn.sigmoid(gl[:, :d]) * _dot(a_ref[...], wb_ref[0])
         + jax.nn.sigmoid(gl[:, d:2 * d]) * _dot(b_ref[...], wb_ref[1])
         + jax.nn.sigmoid(gl[:, 2 * d:]) * _dot(c_ref[...], wb_ref[2]))
    mix = _dot(m.astype(BF16), wo_ref[...])
    y = _layer_norm(DEEPNORM_ALPHA * x_ref[...] + mix, g_ref[...], bt_ref[...])
    xo_ref[...] = y
    xb_ref[...] = y.astype(BF16)


def _merge(a, b, c, proj, x, w_branch, w_out, ln_g, ln_b, *, tm):
    t = x.shape[0]
    br = pl.BlockSpec((tm, BRANCH_W), lambda i: (i, 0))
    xs = pl.BlockSpec((tm, D_MODEL), lambda i: (i, 0))

    def full(shape):
        return pl.BlockSpec(shape, lambda i: (0,) * len(shape))

    return pl.pallas_call(
        _merge_kernel,
        grid=(t // tm,),
        in_specs=[br, br, br,
                  pl.BlockSpec((tm, 3 * D_MODEL), lambda i: (i, COL_GL // (3 * D_MODEL))),
                  xs, full((3, BRANCH_W, D_MODEL)), full((D_MODEL, D_MODEL)),
                  full((1, D_MODEL)), full((1, D_MODEL))],
        out_specs=[xs, xs],
        out_shape=[jax.ShapeDtypeStruct((t, D_MODEL), F32),
                   jax.ShapeDtypeStruct((t, D_MODEL), BF16)],
        compiler_params=_params("parallel"),
        name="merge_ln1",
    )(a, b, c, proj, x, w_branch, w_out, ln_g, ln_b)


def _ffn_kernel(xb_ref, x_ref, w1_ref, w2_ref, g_ref, bt_ref, xo_ref, xbo_ref, acc_ref):
    f = pl.program_id(1)
    h = jnp.square(jnp.maximum(_dot(xb_ref[...], w1_ref[...]), 0.0)).astype(BF16)
    part = _dot(h, w2_ref[...])

    @pl.when(f == 0)
    def _():
        acc_ref[...] = part

    @pl.when(f > 0)
    def _():
        acc_ref[...] += part

    @pl.when(f == pl.num_programs(1) - 1)
    def _():
        y = _layer_norm(DEEPNORM_ALPHA * x_ref[...] + acc_ref[...], g_ref[...], bt_ref[...])
        xo_ref[...] = y
        xbo_ref[...] = y.astype(BF16)


def _ffn(xb, x, w1, w2, ln_g, ln_b, *, tm, tf):
    t = x.shape[0]
    xs = pl.BlockSpec((tm, D_MODEL), lambda i, f: (i, 0))
    vec = pl.BlockSpec((1, D_MODEL), lambda i, f: (0, 0))
    return pl.pallas_call(
        _ffn_kernel,
        grid=(t // tm, D_FF // tf),
        in_specs=[xs, xs,
                  pl.BlockSpec((D_MODEL, tf), lambda i, f: (0, f)),
                  pl.BlockSpec((tf, D_MODEL), lambda i, f: (f, 0)),
                  vec, vec],
        out_specs=[xs, xs],
        out_shape=[jax.ShapeDtypeStruct((t, D_MODEL), F32),
                   jax.ShapeDtypeStruct((t, D_MODEL), BF16)],
        scratch_shapes=[pltpu.VMEM((tm, D_MODEL), F32)],
        compiler_params=_params("parallel", "arbitrary"),
        name="ffn_ln2",
    )(xb, x, w1, w2, ln_g, ln_b)


def _split_w_in(w_in_l):
    sizes = [512, 512, 512, 512, GLA_GATE_RANK, 1024, 512, 512, 512, 3 * D_MODEL]
    offs = np.concatenate([[0], np.cumsum(sizes)])
    gq, gk, gv, gg, glr, gz, mq, mk, mv, gl = [w_in_l[:, offs[i]:offs[i + 1]] for i in range(10)]
    w_main = jnp.concatenate([gl, gq, gk, gv, gg, gz, mq, mk, mv], axis=1).astype(BF16)
    w_lr = jnp.pad(glr, ((0, 0), (0, LANES - GLA_GATE_RANK))).astype(BF16)
    return w_main, w_lr


def kernel(x, positions, w_in, w_gate_up, b_gate, gla_norm_w, gmlp_ln_g, gmlp_ln_b, gmlp_w_s,
           gmlp_b_s, w_branch, w_out, ln1_g, ln1_b, w_ff1, w_ff2, ln2_g, ln2_b):
    batch, seq, d = x.shape
    t = batch * seq
    depth = w_in.shape[0]
    assert d == D_MODEL and seq % 512 == 0

    half = ROPE_DIMS // 2
    inv = 1.0 / (ROPE_THETA ** (jnp.arange(half, dtype=F32) * (2.0 / ROPE_DIMS)))
    inv_row = jnp.concatenate([inv, inv, jnp.zeros((LANES - ROPE_DIMS,), F32)])[None, :]
    cos, s1, s2 = _rope_tables(positions.reshape(t, 1), inv_row, rows=min(1024, seq))

    xf = x.reshape(t, d)
    xb = xf.astype(BF16)
    for l in range(depth):
        w_main, w_lr = _split_w_in(w_in[l])
        w_gu = jnp.pad(w_gate_up[l], ((0, LANES - GLA_GATE_RANK), (0, 0))).astype(BF16)
        bs_full = jnp.repeat(gmlp_b_s[l].T, GMLP_GROUP_CH, axis=1)

        proj = _matmul(xb, w_main, tm=1024, tn=1536, out_dtype=BF16)
        a = _gla(xb, proj, w_lr, w_gu, b_gate[l][None, :], gla_norm_w[l][None, :],
                 batch=batch, rows=512)
        b = _gmlp(proj, gmlp_ln_g[l][None, :], gmlp_ln_b[l][None, :], gmlp_w_s[l], bs_full,
                  rows=512)
        q_rot, k_rot, k_mean = _moba_prep(proj, cos, s1, s2)
        c = _moba_flash(q_rot.reshape(batch, seq, -1), k_rot.reshape(batch, seq, -1),
                        proj.reshape(batch, seq, -1),
                        k_mean.reshape(batch, seq // MOBA_BLOCK, -1))
        xf, xb = _merge(a, b, c.reshape(t, -1), proj, xf, w_branch[l].astype(BF16),
                        w_out[l].astype(BF16), ln1_g[l][None, :], ln1_b[l][None, :], tm=512)
        xf, xb = _ffn(xb, xf, w_ff1[l].astype(BF16), w_ff2[l].astype(BF16),
                      ln2_g[l][None, :], ln2_b[l][None, :], tm=1024, tf=512)
    return xf.reshape(batch, seq, d)
```

```python
import functools

import numpy as np
import jax
import jax.numpy as jnp
from jax import lax
from jax.experimental import pallas as pl
from jax.experimental.pallas import tpu as pltpu

D_MODEL = 1024
DEPTH = 4
GLA_HEADS = 4
GLA_DK = 128
GLA_DV = 128
GLA_GATE_RANK = 16
GLA_GATE_NORM = 16.0
GLA_CHUNK = 64
GMLP_GROUPS = 4
GMLP_GROUP_CH = 128
GMLP_WIDTH = 512
GMLP_CHUNK = 128
MOBA_HEADS = 4
MOBA_HD = 128
MOBA_BLOCK = 256
MOBA_TOPK = 3
ROPE_THETA = 500000.0
ROPE_DIMS = MOBA_HD // 4
BRANCH_W = 512
D_FF = 4 * D_MODEL
DEEPNORM_ALPHA = (2 * DEPTH) ** 0.25
LN_EPS = 1e-5
RMS_EPS = 1e-6

LANES = 128
VMEM_LIMIT = 56 * 1024 * 1024

COL_GL = 0
COL_GQ = 3072
COL_GK = 3584
COL_GV = 4096
COL_GG = 4608
COL_GZ = 5120
COL_MQ = 6144
COL_MK = 6656
COL_MV = 7168
PROJ_COLS = 7680

BF16 = jnp.bfloat16
F32 = jnp.float32


def _dot(a, b):
    return jnp.dot(a, b, preferred_element_type=F32)


def _dot_nt(a, b):
    return lax.dot_general(a, b, (((1,), (1,)), ((), ())), preferred_element_type=F32)


def _dot_tn(a, b):
    return lax.dot_general(a, b, (((0,), (0,)), ((), ())), preferred_element_type=F32)


def _params(*sem):
    return pltpu.CompilerParams(dimension_semantics=sem, vmem_limit_bytes=VMEM_LIMIT)


def _layer_norm(y, g, b):
    mu = jnp.mean(y, axis=-1, keepdims=True)
    yc = y - mu
    var = jnp.mean(yc * yc, axis=-1, keepdims=True)
    return yc * lax.rsqrt(var + LN_EPS) * g + b


def _matmul_kernel(x_ref, w_ref, o_ref):
    o_ref[...] = _dot(x_ref[...], w_ref[...]).astype(o_ref.dtype)


def _matmul(x, w, *, tm, tn, out_dtype):
    m, k = x.shape
    n = w.shape[1]
    return pl.pallas_call(
        _matmul_kernel,
        grid=(m // tm, n // tn),
        in_specs=[pl.BlockSpec((tm, k), lambda i, j: (i, 0)),
                  pl.BlockSpec((k, tn), lambda i, j: (0, j))],
        out_specs=pl.BlockSpec((tm, tn), lambda i, j: (i, j)),
        out_shape=jax.ShapeDtypeStruct((m, n), out_dtype),
        compiler_params=_params("parallel", "parallel"),
        name="in_proj",
    )(x, w)


def _gla_kernel(x_ref, q_ref, k_ref, v_ref, g_ref, wlr_ref, wgu_ref, bg_ref, nw_ref,
                o_ref, st_ref, la_ref):
    rows = x_ref.shape[0]
    C = GLA_CHUNK

    @pl.when(pl.program_id(1) == 0)
    def _():
        st_ref[...] = jnp.zeros_like(st_ref)

    lr = _dot(x_ref[...], wlr_ref[...]).astype(BF16)
    z = _dot(lr, wgu_ref[...]) + bg_ref[...]
    la_ref[...] = (jnp.minimum(z, 0.0) - jnp.log1p(jnp.exp(-jnp.abs(z)))) / GLA_GATE_NORM

    row = lax.broadcasted_iota(jnp.int32, (C, C), 0)
    col = lax.broadcasted_iota(jnp.int32, (C, C), 1)
    causal = row >= col
    tril = causal.astype(BF16)
    nw = nw_ref[...]

    def chunk(c, carry):
        r0 = pl.multiple_of(c * C, C)
        la = la_ref[pl.ds(r0, C), :]
        hi = la.astype(BF16)
        r1 = la - hi.astype(F32)
        mid = r1.astype(BF16)
        lo = (r1 - mid.astype(F32)).astype(BF16)
        bcum = _dot(tril, hi) + _dot(tril, mid) + _dot(tril, lo)
        b_end = bcum[C - 1:C, :]
        q = q_ref[pl.ds(r0, C), :].astype(F32) * (GLA_DK ** -0.5)
        k = k_ref[pl.ds(r0, C), :].astype(F32)
        v = v_ref[pl.ds(r0, C), :].astype(BF16)
        g = g_ref[pl.ds(r0, C), :].astype(F32)
        q_dec = (q * jnp.exp(bcum)).astype(BF16)
        k_inv = (k * jnp.exp(-bcum)).astype(BF16)
        k_end = (k * jnp.exp(b_end - bcum)).astype(BF16)
        decay = jnp.exp(b_end)
        for h in range(GLA_HEADS):
            sl = slice(h * LANES, (h + 1) * LANES)
            attn = jnp.where(causal, _dot_nt(q_dec[:, sl], k_inv[:, sl]), 0.0).astype(BF16)
            st = st_ref[h]
            o = _dot(attn, v[:, sl]) + _dot_nt(q_dec[:, sl], st.astype(BF16))
            st_ref[h] = decay[:, sl] * st + _dot_tn(v[:, sl], k_end[:, sl])
            o = o * lax.rsqrt(jnp.mean(o * o, axis=-1, keepdims=True) + RMS_EPS) * nw
            gh = g[:, sl]
            o = o * (gh * jax.nn.sigmoid(gh))
            o_ref[pl.ds(r0, C), sl] = o.astype(o_ref.dtype)
        return carry

    lax.fori_loop(0, rows // C, chunk, 0)


def _gla(xb, proj, w_lr, w_gu, b_gate, norm_w, *, batch, rows):
    t = xb.shape[0]
    steps = t // batch // rows
    w = GLA_HEADS * GLA_DK

    def col(off):
        return pl.BlockSpec((rows, w), lambda b, s: (b * steps + s, off // w))

    def full(shape):
        return pl.BlockSpec(shape, lambda b, s: (0,) * len(shape))

    return pl.pallas_call(
        _gla_kernel,
        grid=(batch, steps),
        in_specs=[pl.BlockSpec((rows, D_MODEL), lambda b, s: (b * steps + s, 0)),
                  col(COL_GQ), col(COL_GK), col(COL_GV), col(COL_GG),
                  full((D_MODEL, LANES)), full((LANES, w)), full((1, w)), full((1, GLA_DV))],
        out_specs=pl.BlockSpec((rows, w), lambda b, s: (b * steps + s, 0)),
        out_shape=jax.ShapeDtypeStruct((t, w), BF16),
        scratch_shapes=[pltpu.VMEM((GLA_HEADS, GLA_DV, GLA_DK), F32),
                        pltpu.VMEM((rows, w), F32)],
        compiler_params=_params("arbitrary", "arbitrary"),
        name="gla",
    )(xb, proj, proj, proj, proj, w_lr, w_gu, b_gate, norm_w)


def _gmlp_kernel(z_ref, lng_ref, lnb_ref, ws_ref, bs_ref, o_ref):
    rows = z_ref.shape[0]
    C = GMLP_CHUNK
    z = z_ref[...].astype(F32)
    z = 0.5 * z * (1.0 + lax.erf(z * np.float32(np.sqrt(0.5))))
    u = z[:, :GMLP_WIDTH]
    v = _layer_norm(z[:, GMLP_WIDTH:], lng_ref[...], lnb_ref[...]).astype(BF16)
    row = lax.broadcasted_iota(jnp.int32, (C, C), 0)
    col = lax.broadcasted_iota(jnp.int32, (C, C), 1)
    tril = (row >= col).astype(F32)
    for g in range(GMLP_GROUPS):
        sl = slice(g * GMLP_GROUP_CH, (g + 1) * GMLP_GROUP_CH)
        w = (ws_ref[g] * tril).astype(BF16)
        for c in range(rows // C):
            rs = slice(c * C, (c + 1) * C)
            vs = _dot(w, v[rs, sl]) + bs_ref[:, sl]
            o_ref[rs, sl] = (u[rs, sl] * vs).astype(o_ref.dtype)


def _gmlp(proj, ln_g, ln_b, w_s, bs_full, *, rows):
    t = proj.shape[0]

    def full(shape):
        return pl.BlockSpec(shape, lambda i: (0,) * len(shape))

    return pl.pallas_call(
        _gmlp_kernel,
        grid=(t // rows,),
        in_specs=[pl.BlockSpec((rows, 2 * GMLP_WIDTH), lambda i: (i, COL_GZ // (2 * GMLP_WIDTH))),
                  full((1, GMLP_WIDTH)), full((1, GMLP_WIDTH)),
                  full((GMLP_GROUPS, GMLP_CHUNK, GMLP_CHUNK)), full((GMLP_CHUNK, GMLP_WIDTH))],
        out_specs=pl.BlockSpec((rows, GMLP_WIDTH), lambda i: (i, 0)),
        out_shape=jax.ShapeDtypeStruct((t, GMLP_WIDTH), BF16),
        compiler_params=_params("parallel"),
        name="gmlp",
    )(proj, ln_g, ln_b, w_s, bs_full)


def _rope_table_kernel(pos_ref, inv_ref, c_ref, s1_ref, s2_ref):
    half = ROPE_DIMS // 2
    ang = pos_ref[...].astype(F32) * inv_ref[...]
    lane = lax.broadcasted_iota(jnp.int32, ang.shape, 1)
    cos = jnp.cos(ang)
    sin = jnp.sin(ang)
    c_ref[...] = jnp.where(lane < ROPE_DIMS, cos, 1.0)
    s1_ref[...] = jnp.where(lane < half, -sin, 0.0)
    s2_ref[...] = jnp.where((lane >= half) & (lane < ROPE_DIMS), sin, 0.0)


def _rope_tables(pos_col, inv_row, *, rows):
    t = pos_col.shape[0]
    spec = pl.BlockSpec((rows, LANES), lambda i: (i, 0))
    shp = jax.ShapeDtypeStruct((t, LANES), F32)
    return pl.pallas_call(
        _rope_table_kernel,
        grid=(t // rows,),
        in_specs=[pl.BlockSpec((rows, 1), lambda i: (i, 0)),
                  pl.BlockSpec((1, LANES), lambda i: (0, 0))],
        out_specs=[spec, spec, spec],
        out_shape=[shp, shp, shp],
        compiler_params=_params("parallel"),
        name="rope_tables",
    )(pos_col, inv_row)


def _moba_prep_kernel(q_ref, k_ref, v_ref, c_ref, s1_ref, s2_ref, qt_ref, ko_ref, vt_ref, km_ref):
    half = ROPE_DIMS // 2
    cos = c_ref[...]
    s1 = s1_ref[...]
    s2 = s2_ref[...]
    q_scale = np.float32(MOBA_HD ** -0.5 * np.log2(np.e))

    def rope(t):
        return (t * cos + pltpu.roll(t, LANES - half, 1) * s1 + pltpu.roll(t, half, 1) * s2)

    for h in range(MOBA_HEADS):
        sl = slice(h * LANES, (h + 1) * LANES)
        qr = rope(q_ref[:, sl].astype(F32)) * q_scale
        qt_ref[sl, :] = qr.T.astype(qt_ref.dtype)
        kr = rope(k_ref[:, sl].astype(F32))
        ko_ref[:, sl] = kr.astype(ko_ref.dtype)
        km_ref[:, sl] = jnp.mean(kr, axis=0, keepdims=True)
        vt_ref[sl, :] = v_ref[:, sl].astype(F32).T.astype(vt_ref.dtype)


def _moba_prep(proj3, cos3, s13, s23):
    batch, seq, _ = proj3.shape
    w = MOBA_HEADS * MOBA_HD
    nb = seq // MOBA_BLOCK
    tab = pl.BlockSpec((None, MOBA_BLOCK, LANES), lambda b, i: (b, i, 0))

    def col(off):
        return pl.BlockSpec((None, MOBA_BLOCK, w), lambda b, i: (b, i, off // w))

    tr = pl.BlockSpec((None, w, MOBA_BLOCK), lambda b, i: (b, 0, i))
    return pl.pallas_call(
        _moba_prep_kernel,
        grid=(batch, nb),
        in_specs=[col(COL_MQ), col(COL_MK), col(COL_MV), tab, tab, tab],
        out_specs=[tr, pl.BlockSpec((None, MOBA_BLOCK, w), lambda b, i: (b, i, 0)), tr,
                   pl.BlockSpec((None, None, 1, w), lambda b, i: (b, i, 0, 0))],
        out_shape=[jax.ShapeDtypeStruct((batch, w, seq), BF16),
                   jax.ShapeDtypeStruct((batch, seq, w), BF16),
                   jax.ShapeDtypeStruct((batch, w, seq), BF16),
                   jax.ShapeDtypeStruct((batch, nb, 1, w), F32)],
        compiler_params=_params("parallel", "parallel"),
        name="moba_prep",
    )(proj3, proj3, proj3, cos3, s13, s23)


def _moba_flash_kernel(qt_ref, k_ref, vt_ref, km_ref, o_ref, bias_ref, *, heads):
    BLK = MOBA_BLOCK
    qi = pl.program_id(2)
    n_blk = km_ref.shape[0]
    r0 = pl.multiple_of(qi * BLK, BLK)
    blk = lax.broadcasted_iota(jnp.int32, (n_blk, BLK), 0)
    blk_f = blk.astype(F32)
    key = lax.broadcasted_iota(jnp.int32, (BLK, BLK), 0)
    qry = lax.broadcasted_iota(jnp.int32, (BLK, BLK), 1)
    hs = [slice(h * MOBA_HD, (h + 1) * MOBA_HD) for h in range(heads)]

    carry0 = []
    for h in range(heads):
        qt = qt_ref[hs[h], :]
        km = km_ref[:, hs[h]]
        km_hi = km.astype(BF16)
        km_lo = (km - km_hi.astype(F32)).astype(BF16)
        gate = _dot(km_hi, qt) + _dot(km_lo, qt)
        cand = blk < qi
        sel = jnp.zeros(gate.shape, jnp.bool_)
        for r in range(min(MOBA_TOPK, n_blk)):
            g_eff = jnp.where(cand, gate, -jnp.inf)
            best = jnp.max(g_eff, axis=0, keepdims=True)
            idx = jnp.min(jnp.where(cand & (g_eff == best), blk_f, float(n_blk)),
                          axis=0, keepdims=True)
            idx = jnp.where(r < qi, idx, float(n_blk))
            sel = sel | (blk_f == idx)
            cand = cand & (blk_f != idx)
        bias_ref[h] = jnp.where(sel, 0.0, -jnp.inf)

        s = _dot(k_ref[pl.ds(r0, BLK), hs[h]], qt)
        s = jnp.where(key <= qry, s, -jnp.inf)
        m = jnp.max(s, axis=0, keepdims=True)
        p = jnp.exp2(s - m)
        l = jnp.sum(p, axis=0, keepdims=True)
        acc = _dot(vt_ref[hs[h], pl.ds(r0, BLK)], p.astype(BF16))
        carry0 += [m, l, acc]

    def body(j, carry):
        c0 = pl.multiple_of(j * BLK, BLK)
        out = []
        n0 = pl.multiple_of(jnp.minimum(j + 1, n_blk - 1) * BLK, BLK)
        for h in range(heads):
            m, l, acc, s = carry[4 * h:4 * h + 4]
            s_next = _dot(k_ref[pl.ds(n0, BLK), hs[h]], qt_ref[hs[h], :])
            bias = bias_ref[h, pl.ds(j, 1), :]
            m_new = jnp.maximum(m, jnp.max(s, axis=0, keepdims=True) + bias)
            alpha = jnp.exp2(m - m_new)
            p = jnp.exp2(s - (m_new - bias))
            l = alpha * l + jnp.sum(p, axis=0, keepdims=True)
            acc = alpha * acc + _dot(vt_ref[hs[h], pl.ds(c0, BLK)], p.astype(BF16))
            out += [m_new, l, acc, s_next]
        return tuple(out)

    carry1 = []
    for h in range(heads):
        carry1 += carry0[3 * h:3 * h + 3] + [_dot(k_ref[pl.ds(0, BLK), hs[h]], qt_ref[hs[h], :])]
    carry = lax.fori_loop(0, qi, body, tuple(carry1))
    for h in range(heads):
        m, l, acc = carry[4 * h:4 * h + 3]
        o_ref[:, hs[h]] = (acc / l).T.astype(o_ref.dtype)


def _moba_flash(q_t, k_rot, v_t, k_mean, *, heads):
    batch, seq, w = k_rot.shape
    n_blk = seq // MOBA_BLOCK
    hw = heads * MOBA_HD
    return pl.pallas_call(
        functools.partial(_moba_flash_kernel, heads=heads),
        grid=(batch, MOBA_HEADS // heads, n_blk),
        in_specs=[pl.BlockSpec((None, hw, MOBA_BLOCK), lambda b, g, i: (b, g, i)),
                  pl.BlockSpec((None, seq, hw), lambda b, g, i: (b, 0, g)),
                  pl.BlockSpec((None, hw, seq), lambda b, g, i: (b, g, 0)),
                  pl.BlockSpec((None, n_blk, hw), lambda b, g, i: (b, 0, g))],
        out_specs=pl.BlockSpec((None, MOBA_BLOCK, hw), lambda b, g, i: (b, i, g)),
        out_shape=jax.ShapeDtypeStruct((batch, seq, w), BF16),
        scratch_shapes=[pltpu.VMEM((heads, n_blk, MOBA_BLOCK), F32)],
        compiler_params=_params("parallel", "parallel", "arbitrary"),
        name="moba_flash",
    )(q_t, k_rot, v_t, k_mean)


def _merge_kernel(a_ref, b_ref, c_ref, gl_ref, x_ref, wb_ref, wo_ref, g_ref, bt_ref,
                  xo_ref, xb_ref):
    gl = gl_ref[...].astype(F32)
    d = D_MODEL
    m = (jax.nn.sigmoid(gl[:, :d]) * _dot(a_ref[...], wb_ref[0])
         + jax.nn.sigmoid(gl[:, d:2 * d]) * _dot(b_ref[...], wb_ref[1])
         + jax.nn.sigmoid(gl[:, 2 * d:]) * _dot(c_ref[...], wb_ref[2]))
    mix = _dot(m.astype(BF16), wo_ref[...])
    y = _layer_norm(DEEPNORM_ALPHA * x_ref[...] + mix, g_ref[...], bt_ref[...])
    xo_ref[...] = y
    xb_ref[...] = y.astype(BF16)


def _merge(a, b, c, proj, x, w_branch, w_out, ln_g, ln_b, *, tm):
    t = x.shape[0]
    br = pl.BlockSpec((tm, BRANCH_W), lambda i: (i, 0))
    xs = pl.BlockSpec((tm, D_MODEL), lambda i: (i, 0))

    def full(shape):
        return pl.BlockSpec(shape, lambda i: (0,) * len(shape))

    return pl.pallas_call(
        _merge_kernel,
        grid=(t // tm,),
        in_specs=[br, br, br,
                  pl.BlockSpec((tm, 3 * D_MODEL), lambda i: (i, COL_GL // (3 * D_MODEL))),
                  xs, full((3, BRANCH_W, D_MODEL)), full((D_MODEL, D_MODEL)),
                  full((1, D_MODEL)), full((1, D_MODEL))],
        out_specs=[xs, xs],
        out_shape=[jax.ShapeDtypeStruct((t, D_MODEL), F32),
                   jax.ShapeDtypeStruct((t, D_MODEL), BF16)],
        compiler_params=_params("parallel"),
        name="merge_ln1",
    )(a, b, c, proj, x, w_branch, w_out, ln_g, ln_b)


def _ffn_kernel(xb_ref, x_ref, w1_ref, w2_ref, g_ref, bt_ref, xo_ref, xbo_ref, acc_ref):
    f = pl.program_id(1)
    h = jnp.square(jnp.maximum(_dot(xb_ref[...], w1_ref[...]), 0.0)).astype(BF16)
    part = _dot(h, w2_ref[...])

    @pl.when(f == 0)
    def _():
        acc_ref[...] = part

    @pl.when(f > 0)
    def _():
        acc_ref[...] += part

    @pl.when(f == pl.num_programs(1) - 1)
    def _():
        y = _layer_norm(DEEPNORM_ALPHA * x_ref[...] + acc_ref[...], g_ref[...], bt_ref[...])
        xo_ref[...] = y
        xbo_ref[...] = y.astype(BF16)


def _ffn(xb, x, w1, w2, ln_g, ln_b, *, tm, tf):
    t = x.shape[0]
    xs = pl.BlockSpec((tm, D_MODEL), lambda i, f: (i, 0))
    vec = pl.BlockSpec((1, D_MODEL), lambda i, f: (0, 0))
    return pl.pallas_call(
        _ffn_kernel,
        grid=(t // tm, D_FF // tf),
        in_specs=[xs, xs,
                  pl.BlockSpec((D_MODEL, tf), lambda i, f: (0, f)),
                  pl.BlockSpec((tf, D_MODEL), lambda i, f: (f, 0)),
                  vec, vec],
        out_specs=[xs, xs],
        out_shape=[jax.ShapeDtypeStruct((t, D_MODEL), F32),
                   jax.ShapeDtypeStruct((t, D_MODEL), BF16)],
        scratch_shapes=[pltpu.VMEM((tm, D_MODEL), F32)],
        compiler_params=_params("parallel", "arbitrary"),
        name="ffn_ln2",
    )(xb, x, w1, w2, ln_g, ln_b)


def _split_w_in(w_in_l):
    sizes = [512, 512, 512, 512, GLA_GATE_RANK, 1024, 512, 512, 512, 3 * D_MODEL]
    offs = np.concatenate([[0], np.cumsum(sizes)])
    gq, gk, gv, gg, glr, gz, mq, mk, mv, gl = [w_in_l[:, offs[i]:offs[i + 1]] for i in range(10)]
    w_main = jnp.concatenate([gl, gq, gk, gv, gg, gz, mq, mk, mv], axis=1).astype(BF16)
    w_lr = jnp.pad(glr, ((0, 0), (0, LANES - GLA_GATE_RANK))).astype(BF16)
    return w_main, w_lr


def kernel(x, positions, w_in, w_gate_up, b_gate, gla_norm_w, gmlp_ln_g, gmlp_ln_b, gmlp_w_s,
           gmlp_b_s, w_branch, w_out, ln1_g, ln1_b, w_ff1, w_ff2, ln2_g, ln2_b):
    batch, seq, d = x.shape
    t = batch * seq
    depth = w_in.shape[0]
    assert d == D_MODEL and seq % 512 == 0

    half = ROPE_DIMS // 2
    inv = 1.0 / (ROPE_THETA ** (jnp.arange(half, dtype=F32) * (2.0 / ROPE_DIMS)))
    inv_row = jnp.concatenate([inv, inv, jnp.zeros((LANES - ROPE_DIMS,), F32)])[None, :]
    cos, s1, s2 = [a.reshape(batch, seq, LANES) for a in
                   _rope_tables(positions.reshape(t, 1), inv_row, rows=min(1024, seq))]

    xf = x.reshape(t, d)
    xb = xf.astype(BF16)
    for l in range(depth):
        w_main, w_lr = _split_w_in(w_in[l])
        w_gu = jnp.pad(w_gate_up[l], ((0, LANES - GLA_GATE_RANK), (0, 0))).astype(BF16)
        bs_full = jnp.repeat(gmlp_b_s[l].T, GMLP_GROUP_CH, axis=1)

        proj = _matmul(xb, w_main, tm=1024, tn=1536, out_dtype=BF16)
        a = _gla(xb, proj, w_lr, w_gu, b_gate[l][None, :], gla_norm_w[l][None, :],
                 batch=batch, rows=512)
        b = _gmlp(proj, gmlp_ln_g[l][None, :], gmlp_ln_b[l][None, :], gmlp_w_s[l], bs_full,
                  rows=512)
        q_t, k_rot, v_t, k_mean = _moba_prep(proj.reshape(batch, seq, -1), cos, s1, s2)
        c = _moba_flash(q_t, k_rot, v_t, k_mean.reshape(batch, seq // MOBA_BLOCK, -1), heads=4)
        xf, xb = _merge(a, b, c.reshape(t, -1), proj, xf, w_branch[l].astype(BF16),
                        w_out[l].astype(BF16), ln1_g[l][None, :], ln1_b[l][None, :], tm=512)
        xf, xb = _ffn(xb, xf, w_ff1[l].astype(BF16), w_ff2[l].astype(BF16),
                      ln2_g[l][None, :], ln2_b[l][None, :], tm=1024, tf=1024)
    return xf.reshape(batch, seq, d)
```

```python
import functools

import numpy as np
import jax
import jax.numpy as jnp
from jax import lax
from jax.experimental import pallas as pl
from jax.experimental.pallas import tpu as pltpu

D_MODEL = 1024
DEPTH = 4
GLA_HEADS = 4
GLA_DK = 128
GLA_DV = 128
GLA_GATE_RANK = 16
GLA_GATE_NORM = 16.0
GLA_CHUNK = 64
GMLP_GROUPS = 4
GMLP_GROUP_CH = 128
GMLP_WIDTH = 512
GMLP_CHUNK = 128
MOBA_HEADS = 4
MOBA_HD = 128
MOBA_BLOCK = 256
MOBA_TOPK = 3
ROPE_THETA = 500000.0
ROPE_DIMS = MOBA_HD // 4
BRANCH_W = 512
D_FF = 4 * D_MODEL
DEEPNORM_ALPHA = (2 * DEPTH) ** 0.25
LN_EPS = 1e-5
RMS_EPS = 1e-6

LANES = 128
VMEM_LIMIT = 56 * 1024 * 1024

COL_GL = 0
COL_GQ = 3072
COL_GK = 3584
COL_GV = 4096
COL_GG = 4608
COL_GZ = 5120
COL_MQ = 6144
COL_MK = 6656
COL_MV = 7168
PROJ_COLS = 7680

BF16 = jnp.bfloat16
F32 = jnp.float32


def _dot(a, b):
    return jnp.dot(a, b, preferred_element_type=F32)


def _dot_nt(a, b):
    return lax.dot_general(a, b, (((1,), (1,)), ((), ())), preferred_element_type=F32)


def _dot_tn(a, b):
    return lax.dot_general(a, b, (((0,), (0,)), ((), ())), preferred_element_type=F32)


def _params(*sem):
    return pltpu.CompilerParams(dimension_semantics=sem, vmem_limit_bytes=VMEM_LIMIT)


def _layer_norm(y, g, b):
    mu = jnp.mean(y, axis=-1, keepdims=True)
    yc = y - mu
    var = jnp.mean(yc * yc, axis=-1, keepdims=True)
    return yc * lax.rsqrt(var + LN_EPS) * g + b


def _matmul_kernel(x_ref, w_ref, o_ref):
    o_ref[...] = _dot(x_ref[...], w_ref[...]).astype(o_ref.dtype)


def _matmul(x, w, *, tm, tn, out_dtype):
    m, k = x.shape
    n = w.shape[1]
    return pl.pallas_call(
        _matmul_kernel,
        grid=(m // tm, n // tn),
        in_specs=[pl.BlockSpec((tm, k), lambda i, j: (i, 0)),
                  pl.BlockSpec((k, tn), lambda i, j: (0, j))],
        out_specs=pl.BlockSpec((tm, tn), lambda i, j: (i, j)),
        out_shape=jax.ShapeDtypeStruct((m, n), out_dtype),
        compiler_params=_params("parallel", "parallel"),
        name="in_proj",
    )(x, w)


def _gla_kernel(x_ref, q_ref, k_ref, v_ref, g_ref, wlr_ref, wgu_ref, bg_ref, nw_ref,
                o_ref, st_ref, la_ref):
    rows = x_ref.shape[0]
    C = GLA_CHUNK

    @pl.when(pl.program_id(1) == 0)
    def _():
        st_ref[...] = jnp.zeros_like(st_ref)

    lr = _dot(x_ref[...], wlr_ref[...]).astype(BF16)
    z = _dot(lr, wgu_ref[...]) + bg_ref[...]
    la_ref[...] = (jnp.minimum(z, 0.0) - jnp.log1p(jnp.exp(-jnp.abs(z)))) / GLA_GATE_NORM

    row = lax.broadcasted_iota(jnp.int32, (C, C), 0)
    col = lax.broadcasted_iota(jnp.int32, (C, C), 1)
    causal = row >= col
    tril = causal.astype(BF16)
    nw = nw_ref[...]

    nc = rows // C
    hs = [slice(h * LANES, (h + 1) * LANES) for h in range(GLA_HEADS)]
    rs = [slice(c * C, (c + 1) * C) for c in range(nc)]

    q_dec, k_inv, k_end, decay, vv = [], [], [], [], []
    for c in range(nc):
        la = la_ref[rs[c], :]
        hi = la.astype(BF16)
        r1 = la - hi.astype(F32)
        mid = r1.astype(BF16)
        lo = (r1 - mid.astype(F32)).astype(BF16)
        bcum = _dot(tril, hi) + _dot(tril, mid) + _dot(tril, lo)
        b_end = bcum[C - 1:C, :]
        q = q_ref[rs[c], :].astype(F32) * (GLA_DK ** -0.5)
        k = k_ref[rs[c], :].astype(F32)
        q_dec.append((q * jnp.exp(bcum)).astype(BF16))
        k_inv.append((k * jnp.exp(-bcum)).astype(BF16))
        k_end.append((k * jnp.exp(b_end - bcum)).astype(BF16))
        decay.append(jnp.exp(b_end))
        vv.append(v_ref[rs[c], :].astype(BF16))

    attn = [[jnp.where(causal, _dot_nt(q_dec[c][:, sl], k_inv[c][:, sl]), 0.0).astype(BF16)
             for sl in hs] for c in range(nc)]
    kv = [[_dot_tn(vv[c][:, sl], k_end[c][:, sl]) for sl in hs] for c in range(nc)]
    o_intra = [[_dot(attn[c][h], vv[c][:, hs[h]]) for h in range(GLA_HEADS)] for c in range(nc)]

    st = [st_ref[h] for h in range(GLA_HEADS)]
    for c in range(nc):
        g = g_ref[rs[c], :].astype(F32)
        for h, sl in enumerate(hs):
            o = o_intra[c][h] + _dot_nt(q_dec[c][:, sl], st[h].astype(BF16))
            st[h] = decay[c][:, sl] * st[h] + kv[c][h]
            o = o * lax.rsqrt(jnp.mean(o * o, axis=-1, keepdims=True) + RMS_EPS) * nw
            gh = g[:, sl]
            o = o * (gh * jax.nn.sigmoid(gh))
            o_ref[rs[c], sl] = o.astype(o_ref.dtype)
    for h in range(GLA_HEADS):
        st_ref[h] = st[h]


def _gla(xb, proj, w_lr, w_gu, b_gate, norm_w, *, batch, rows):
    t = xb.shape[0]
    steps = t // batch // rows
    w = GLA_HEADS * GLA_DK

    def col(off):
        return pl.BlockSpec((rows, w), lambda b, s: (b * steps + s, off // w))

    def full(shape):
        return pl.BlockSpec(shape, lambda b, s: (0,) * len(shape))

    return pl.pallas_call(
        _gla_kernel,
        grid=(batch, steps),
        in_specs=[pl.BlockSpec((rows, D_MODEL), lambda b, s: (b * steps + s, 0)),
                  col(COL_GQ), col(COL_GK), col(COL_GV), col(COL_GG),
                  full((D_MODEL, LANES)), full((LANES, w)), full((1, w)), full((1, GLA_DV))],
        out_specs=pl.BlockSpec((rows, w), lambda b, s: (b * steps + s, 0)),
        out_shape=jax.ShapeDtypeStruct((t, w), BF16),
        scratch_shapes=[pltpu.VMEM((GLA_HEADS, GLA_DV, GLA_DK), F32),
                        pltpu.VMEM((rows, w), F32)],
        compiler_params=_params("arbitrary", "arbitrary"),
        name="gla",
    )(xb, proj, proj, proj, proj, w_lr, w_gu, b_gate, norm_w)


def _gmlp_kernel(z_ref, lng_ref, lnb_ref, ws_ref, bs_ref, o_ref):
    rows = z_ref.shape[0]
    C = GMLP_CHUNK
    z = z_ref[...].astype(F32)
    z = 0.5 * z * (1.0 + lax.erf(z * np.float32(np.sqrt(0.5))))
    u = z[:, :GMLP_WIDTH]
    v = _layer_norm(z[:, GMLP_WIDTH:], lng_ref[...], lnb_ref[...]).astype(BF16)
    row = lax.broadcasted_iota(jnp.int32, (C, C), 0)
    col = lax.broadcasted_iota(jnp.int32, (C, C), 1)
    tril = (row >= col).astype(F32)
    for g in range(GMLP_GROUPS):
        sl = slice(g * GMLP_GROUP_CH, (g + 1) * GMLP_GROUP_CH)
        w = (ws_ref[g] * tril).astype(BF16)
        for c in range(rows // C):
            rs = slice(c * C, (c + 1) * C)
            vs = _dot(w, v[rs, sl]) + bs_ref[:, sl]
            o_ref[rs, sl] = (u[rs, sl] * vs).astype(o_ref.dtype)


def _gmlp(proj, ln_g, ln_b, w_s, bs_full, *, rows):
    t = proj.shape[0]

    def full(shape):
        return pl.BlockSpec(shape, lambda i: (0,) * len(shape))

    return pl.pallas_call(
        _gmlp_kernel,
        grid=(t // rows,),
        in_specs=[pl.BlockSpec((rows, 2 * GMLP_WIDTH), lambda i: (i, COL_GZ // (2 * GMLP_WIDTH))),
                  full((1, GMLP_WIDTH)), full((1, GMLP_WIDTH)),
                  full((GMLP_GROUPS, GMLP_CHUNK, GMLP_CHUNK)), full((GMLP_CHUNK, GMLP_WIDTH))],
        out_specs=pl.BlockSpec((rows, GMLP_WIDTH), lambda i: (i, 0)),
        out_shape=jax.ShapeDtypeStruct((t, GMLP_WIDTH), BF16),
        compiler_params=_params("parallel"),
        name="gmlp",
    )(proj, ln_g, ln_b, w_s, bs_full)


def _rope_table_kernel(pos_ref, inv_ref, c_ref, s1_ref, s2_ref):
    half = ROPE_DIMS // 2
    ang = pos_ref[...].astype(F32) * inv_ref[...]
    lane = lax.broadcasted_iota(jnp.int32, ang.shape, 1)
    cos = jnp.cos(ang)
    sin = jnp.sin(ang)
    c_ref[...] = jnp.where(lane < ROPE_DIMS, cos, 1.0)
    s1_ref[...] = jnp.where(lane < half, -sin, 0.0)
    s2_ref[...] = jnp.where((lane >= half) & (lane < ROPE_DIMS), sin, 0.0)


def _rope_tables(pos_col, inv_row, *, rows):
    t = pos_col.shape[0]
    spec = pl.BlockSpec((rows, LANES), lambda i: (i, 0))
    shp = jax.ShapeDtypeStruct((t, LANES), F32)
    return pl.pallas_call(
        _rope_table_kernel,
        grid=(t // rows,),
        in_specs=[pl.BlockSpec((rows, 1), lambda i: (i, 0)),
                  pl.BlockSpec((1, LANES), lambda i: (0, 0))],
        out_specs=[spec, spec, spec],
        out_shape=[shp, shp, shp],
        compiler_params=_params("parallel"),
        name="rope_tables",
    )(pos_col, inv_row)


def _moba_prep_kernel(q_ref, k_ref, v_ref, c_ref, s1_ref, s2_ref, qt_ref, ko_ref, vt_ref, km_ref):
    half = ROPE_DIMS // 2
    cos = c_ref[...]
    s1 = s1_ref[...]
    s2 = s2_ref[...]
    q_scale = np.float32(MOBA_HD ** -0.5 * np.log2(np.e))

    def rope(t):
        return (t * cos + pltpu.roll(t, LANES - half, 1) * s1 + pltpu.roll(t, half, 1) * s2)

    for h in range(MOBA_HEADS):
        sl = slice(h * LANES, (h + 1) * LANES)
        qr = rope(q_ref[:, sl].astype(F32)) * q_scale
        qt_ref[sl, :] = qr.T.astype(qt_ref.dtype)
        kr = rope(k_ref[:, sl].astype(F32))
        ko_ref[:, sl] = kr.astype(ko_ref.dtype)
        km_ref[:, sl] = jnp.mean(kr, axis=0, keepdims=True)
        vt_ref[sl, :] = v_ref[:, sl].astype(F32).T.astype(vt_ref.dtype)


def _moba_prep(proj3, cos3, s13, s23):
    batch, seq, _ = proj3.shape
    w = MOBA_HEADS * MOBA_HD
    nb = seq // MOBA_BLOCK
    tab = pl.BlockSpec((None, MOBA_BLOCK, LANES), lambda b, i: (b, i, 0))

    def col(off):
        return pl.BlockSpec((None, MOBA_BLOCK, w), lambda b, i: (b, i, off // w))

    tr = pl.BlockSpec((None, w, MOBA_BLOCK), lambda b, i: (b, 0, i))
    return pl.pallas_call(
        _moba_prep_kernel,
        grid=(batch, nb),
        in_specs=[col(COL_MQ), col(COL_MK), col(COL_MV), tab, tab, tab],
        out_specs=[tr, pl.BlockSpec((None, MOBA_BLOCK, w), lambda b, i: (b, i, 0)), tr,
                   pl.BlockSpec((None, None, 1, w), lambda b, i: (b, i, 0, 0))],
        out_shape=[jax.ShapeDtypeStruct((batch, w, seq), BF16),
                   jax.ShapeDtypeStruct((batch, seq, w), BF16),
                   jax.ShapeDtypeStruct((batch, w, seq), BF16),
                   jax.ShapeDtypeStruct((batch, nb, 1, w), F32)],
        compiler_params=_params("parallel", "parallel"),
        name="moba_prep",
    )(proj3, proj3, proj3, cos3, s13, s23)


def _moba_flash_kernel(qt_ref, k_ref, vt_ref, km_ref, o_ref, bias_ref, s_ref, acc_ref, *, heads):
    BLK = MOBA_BLOCK
    qi = pl.program_id(2)
    n_blk = km_ref.shape[0]
    r0 = pl.multiple_of(qi * BLK, BLK)
    blk = lax.broadcasted_iota(jnp.int32, (n_blk, BLK), 0)
    blk_f = blk.astype(F32)
    key = lax.broadcasted_iota(jnp.int32, (BLK, BLK), 0)
    qry = lax.broadcasted_iota(jnp.int32, (BLK, BLK), 1)
    hs = [slice(h * MOBA_HD, (h + 1) * MOBA_HD) for h in range(heads)]

    carry0 = []
    for h in range(heads):
        qt = qt_ref[hs[h], :]
        km = km_ref[:, hs[h]]
        km_hi = km.astype(BF16)
        km_lo = (km - km_hi.astype(F32)).astype(BF16)
        gate = _dot(km_hi, qt) + _dot(km_lo, qt)
        cand = blk < qi
        sel = jnp.zeros(gate.shape, jnp.bool_)
        for r in range(min(MOBA_TOPK, n_blk)):
            g_eff = jnp.where(cand, gate, -jnp.inf)
            best = jnp.max(g_eff, axis=0, keepdims=True)
            idx = jnp.min(jnp.where(cand & (g_eff == best), blk_f, float(n_blk)),
                          axis=0, keepdims=True)
            idx = jnp.where(r < qi, idx, float(n_blk))
            sel = sel | (blk_f == idx)
            cand = cand & (blk_f != idx)
        bias_ref[h] = jnp.where(sel, 0.0, -jnp.inf)
        s0 = _dot(k_ref[pl.ds(0, BLK), hs[h]], qt)
        s_ref[h] = s0
        acc_ref[h] = jnp.zeros((MOBA_HD, BLK), F32)
        carry0 += [jnp.full((1, BLK), -jnp.inf, F32), jnp.zeros((1, BLK), F32),
                   jnp.max(s0, axis=0, keepdims=True)]

    def body(j, carry):
        c0 = pl.multiple_of(j * BLK, BLK)
        n0 = pl.multiple_of((j + 1) * BLK, BLK)
        out = []
        for h in range(heads):
            m, l, s_max = carry[3 * h:3 * h + 3]
            s_next = _dot(k_ref[pl.ds(n0, BLK), hs[h]], qt_ref[hs[h], :])
            bias = bias_ref[h, pl.ds(j, 1), :]
            m_new = jnp.maximum(m, s_max + bias)
            m_use = jnp.where(m_new == -jnp.inf, 0.0, m_new)
            alpha = jnp.exp2(m - m_use)
            p = jnp.exp2(s_ref[h] - (m_use - bias))
            l = alpha * l + jnp.sum(p, axis=0, keepdims=True)
            acc_ref[h] = alpha * acc_ref[h] + _dot(vt_ref[hs[h], pl.ds(c0, BLK)], p.astype(BF16))
            s_ref[h] = s_next
            out += [m_new, l, jnp.max(s_next, axis=0, keepdims=True)]
        return tuple(out)

    carry = lax.fori_loop(0, qi, body, tuple(carry0))
    for h in range(heads):
        m, l = carry[3 * h:3 * h + 2]
        s = jnp.where(key <= qry, s_ref[h], -jnp.inf)
        m_new = jnp.maximum(m, jnp.max(s, axis=0, keepdims=True))
        alpha = jnp.exp2(m - m_new)
        p = jnp.exp2(s - m_new)
        l = alpha * l + jnp.sum(p, axis=0, keepdims=True)
        acc = alpha * acc_ref[h] + _dot(vt_ref[hs[h], pl.ds(r0, BLK)], p.astype(BF16))
        o_ref[:, hs[h]] = (acc / l).T.astype(o_ref.dtype)


def _moba_flash(q_t, k_rot, v_t, k_mean, *, heads):
    batch, seq, w = k_rot.shape
    n_blk = seq // MOBA_BLOCK
    hw = heads * MOBA_HD
    return pl.pallas_call(
        functools.partial(_moba_flash_kernel, heads=heads),
        grid=(batch, MOBA_HEADS // heads, n_blk),
        in_specs=[pl.BlockSpec((None, hw, MOBA_BLOCK), lambda b, g, i: (b, g, i)),
                  pl.BlockSpec((None, seq, hw), lambda b, g, i: (b, 0, g)),
                  pl.BlockSpec((None, hw, seq), lambda b, g, i: (b, g, 0)),
                  pl.BlockSpec((None, n_blk, hw), lambda b, g, i: (b, 0, g))],
        out_specs=pl.BlockSpec((None, MOBA_BLOCK, hw), lambda b, g, i: (b, i, g)),
        out_shape=jax.ShapeDtypeStruct((batch, seq, w), BF16),
        scratch_shapes=[pltpu.VMEM((heads, n_blk, MOBA_BLOCK), F32),
                        pltpu.VMEM((heads, MOBA_BLOCK, MOBA_BLOCK), F32),
                        pltpu.VMEM((heads, MOBA_HD, MOBA_BLOCK), F32)],
        compiler_params=_params("parallel", "parallel", "arbitrary"),
        name="moba_flash",
    )(q_t, k_rot, v_t, k_mean)


def _merge_kernel(a_ref, b_ref, c_ref, gl_ref, x_ref, wb_ref, wo_ref, g_ref, bt_ref,
                  xo_ref, xb_ref):
    gl = gl_ref[...].astype(F32)
    d = D_MODEL
    m = (jax.nn.sigmoid(gl[:, :d]) * _dot(a_ref[...], wb_ref[0])
         + jax.nn.sigmoid(gl[:, d:2 * d]) * _dot(b_ref[...], wb_ref[1])
         + jax.nn.sigmoid(gl[:, 2 * d:]) * _dot(c_ref[...], wb_ref[2]))
    mix = _dot(m.astype(BF16), wo_ref[...])
    y = _layer_norm(DEEPNORM_ALPHA * x_ref[...] + mix, g_ref[...], bt_ref[...])
    xo_ref[...] = y
    xb_ref[...] = y.astype(BF16)


def _merge(a, b, c, proj, x, w_branch, w_out, ln_g, ln_b, *, tm):
    t = x.shape[0]
    br = pl.BlockSpec((tm, BRANCH_W), lambda i: (i, 0))
    xs = pl.BlockSpec((tm, D_MODEL), lambda i: (i, 0))

    def full(shape):
        return pl.BlockSpec(shape, lambda i: (0,) * len(shape))

    return pl.pallas_call(
        _merge_kernel,
        grid=(t // tm,),
        in_specs=[br, br, br,
                  pl.BlockSpec((tm, 3 * D_MODEL), lambda i: (i, COL_GL // (3 * D_MODEL))),
                  xs, full((3, BRANCH_W, D_MODEL)), full((D_MODEL, D_MODEL)),
                  full((1, D_MODEL)), full((1, D_MODEL))],
        out_specs=[xs, xs],
        out_shape=[jax.ShapeDtypeStruct((t, D_MODEL), F32),
                   jax.ShapeDtypeStruct((t, D_MODEL), BF16)],
        compiler_params=_params("parallel"),
        name="merge_ln1",
    )(a, b, c, proj, x, w_branch, w_out, ln_g, ln_b)


def _ffn_kernel(xb_ref, x_ref, w1_ref, w2_ref, g_ref, bt_ref, xo_ref, xbo_ref, acc_ref):
    f = pl.program_id(1)
    h = jnp.square(jnp.maximum(_dot(xb_ref[...], w1_ref[...]), 0.0)).astype(BF16)
    part = _dot(h, w2_ref[...])

    @pl.when(f == 0)
    def _():
        acc_ref[...] = part

    @pl.when(f > 0)
    def _():
        acc_ref[...] += part

    @pl.when(f == pl.num_programs(1) - 1)
    def _():
        y = _layer_norm(DEEPNORM_ALPHA * x_ref[...] + acc_ref[...], g_ref[...], bt_ref[...])
        xo_ref[...] = y
        xbo_ref[...] = y.astype(BF16)


def _ffn(xb, x, w1, w2, ln_g, ln_b, *, tm, tf):
    t = x.shape[0]
    xs = pl.BlockSpec((tm, D_MODEL), lambda i, f: (i, 0))
    vec = pl.BlockSpec((1, D_MODEL), lambda i, f: (0, 0))
    return pl.pallas_call(
        _ffn_kernel,
        grid=(t // tm, D_FF // tf),
        in_specs=[xs, xs,
                  pl.BlockSpec((D_MODEL, tf), lambda i, f: (0, f)),
                  pl.BlockSpec((tf, D_MODEL), lambda i, f: (f, 0)),
                  vec, vec],
        out_specs=[xs, xs],
        out_shape=[jax.ShapeDtypeStruct((t, D_MODEL), F32),
                   jax.ShapeDtypeStruct((t, D_MODEL), BF16)],
        scratch_shapes=[pltpu.VMEM((tm, D_MODEL), F32)],
        compiler_params=_params("parallel", "arbitrary"),
        name="ffn_ln2",
    )(xb, x, w1, w2, ln_g, ln_b)


def _split_w_in(w_in_l):
    sizes = [512, 512, 512, 512, GLA_GATE_RANK, 1024, 512, 512, 512, 3 * D_MODEL]
    offs = np.concatenate([[0], np.cumsum(sizes)])
    gq, gk, gv, gg, glr, gz, mq, mk, mv, gl = [w_in_l[:, offs[i]:offs[i + 1]] for i in range(10)]
    w_main = jnp.concatenate([gl, gq, gk, gv, gg, gz, mq, mk, mv], axis=1).astype(BF16)
    w_lr = jnp.pad(glr, ((0, 0), (0, LANES - GLA_GATE_RANK))).astype(BF16)
    return w_main, w_lr


def kernel(x, positions, w_in, w_gate_up, b_gate, gla_norm_w, gmlp_ln_g, gmlp_ln_b, gmlp_w_s,
           gmlp_b_s, w_branch, w_out, ln1_g, ln1_b, w_ff1, w_ff2, ln2_g, ln2_b):
    batch, seq, d = x.shape
    t = batch * seq
    depth = w_in.shape[0]
    assert d == D_MODEL and seq % 512 == 0

    half = ROPE_DIMS // 2
    inv = 1.0 / (ROPE_THETA ** (jnp.arange(half, dtype=F32) * (2.0 / ROPE_DIMS)))
    inv_row = jnp.concatenate([inv, inv, jnp.zeros((LANES - ROPE_DIMS,), F32)])[None, :]
    cos, s1, s2 = [a.reshape(batch, seq, LANES) for a in
                   _rope_tables(positions.reshape(t, 1), inv_row, rows=min(1024, seq))]

    xf = x.reshape(t, d)
    xb = xf.astype(BF16)
    for l in range(depth):
        w_main, w_lr = _split_w_in(w_in[l])
        w_gu = jnp.pad(w_gate_up[l], ((0, LANES - GLA_GATE_RANK), (0, 0))).astype(BF16)
        bs_full = jnp.repeat(gmlp_b_s[l].T, GMLP_GROUP_CH, axis=1)

        proj = _matmul(xb, w_main, tm=1024, tn=1536, out_dtype=BF16)
        a = _gla(xb, proj, w_lr, w_gu, b_gate[l][None, :], gla_norm_w[l][None, :],
                 batch=batch, rows=512)
        b = _gmlp(proj, gmlp_ln_g[l][None, :], gmlp_ln_b[l][None, :], gmlp_w_s[l], bs_full,
                  rows=512)
        q_t, k_rot, v_t, k_mean = _moba_prep(proj.reshape(batch, seq, -1), cos, s1, s2)
        c = _moba_flash(q_t, k_rot, v_t, k_mean.reshape(batch, seq // MOBA_BLOCK, -1), heads=4)
        xf, xb = _merge(a, b, c.reshape(t, -1), proj, xf, w_branch[l].astype(BF16),
                        w_out[l].astype(BF16), ln1_g[l][None, :], ln1_b[l][None, :], tm=512)
        xf, xb = _ffn(xb, xf, w_ff1[l].astype(BF16), w_ff2[l].astype(BF16),
                      ln2_g[l][None, :], ln2_b[l][None, :], tm=1024, tf=1024)
    return xf.reshape(batch, seq, d)
```

```python
import functools

import numpy as np
import jax
import jax.numpy as jnp
from jax import lax
from jax.experimental import pallas as pl
from jax.experimental.pallas import tpu as pltpu

D_MODEL = 1024
DEPTH = 4
GLA_HEADS = 4
GLA_DK = 128
GLA_DV = 128
GLA_GATE_RANK = 16
GLA_GATE_NORM = 16.0
GLA_CHUNK = 64
GMLP_GROUPS = 4
GMLP_GROUP_CH = 128
GMLP_WIDTH = 512
GMLP_CHUNK = 128
MOBA_HEADS = 4
MOBA_HD = 128
MOBA_BLOCK = 256
MOBA_TOPK = 3
MOBA_VT_ROWS = MOBA_HD + 16
ROPE_THETA = 500000.0
ROPE_DIMS = MOBA_HD // 4
BRANCH_W = 512
D_FF = 4 * D_MODEL
DEEPNORM_ALPHA = (2 * DEPTH) ** 0.25
LN_EPS = 1e-5
RMS_EPS = 1e-6

LANES = 128
VMEM_LIMIT = 56 * 1024 * 1024

COL_GL = 0
COL_GQ = 3072
COL_GK = 3584
COL_GV = 4096
COL_GG = 4608
COL_GZ = 5120
COL_MQ = 6144
COL_MK = 6656
COL_MV = 7168
PROJ_COLS = 7680

BF16 = jnp.bfloat16
F32 = jnp.float32


def _dot(a, b):
    return jnp.dot(a, b, preferred_element_type=F32)


def _dot_nt(a, b):
    return lax.dot_general(a, b, (((1,), (1,)), ((), ())), preferred_element_type=F32)


def _dot_tn(a, b):
    return lax.dot_general(a, b, (((0,), (0,)), ((), ())), preferred_element_type=F32)


def _params(*sem):
    return pltpu.CompilerParams(dimension_semantics=sem, vmem_limit_bytes=VMEM_LIMIT)


def _layer_norm(y, g, b):
    mu = jnp.mean(y, axis=-1, keepdims=True)
    yc = y - mu
    var = jnp.mean(yc * yc, axis=-1, keepdims=True)
    return yc * lax.rsqrt(var + LN_EPS) * g + b


def _matmul_kernel(x_ref, w_ref, o_ref):
    o_ref[...] = _dot(x_ref[...], w_ref[...]).astype(o_ref.dtype)


def _matmul(x, w, *, tm, tn, out_dtype):
    m, k = x.shape
    n = w.shape[1]
    return pl.pallas_call(
        _matmul_kernel,
        grid=(m // tm, n // tn),
        in_specs=[pl.BlockSpec((tm, k), lambda i, j: (i, 0)),
                  pl.BlockSpec((k, tn), lambda i, j: (0, j))],
        out_specs=pl.BlockSpec((tm, tn), lambda i, j: (i, j)),
        out_shape=jax.ShapeDtypeStruct((m, n), out_dtype),
        compiler_params=_params("parallel", "parallel"),
        name="in_proj",
    )(x, w)


def _gla_kernel(x_ref, q_ref, k_ref, v_ref, g_ref, wlr_ref, wgu_ref, bg_ref, nw_ref,
                o_ref, st_ref, la_ref):
    rows = x_ref.shape[0]
    C = GLA_CHUNK

    @pl.when(pl.program_id(1) == 0)
    def _():
        st_ref[...] = jnp.zeros_like(st_ref)

    lr = _dot(x_ref[...], wlr_ref[...]).astype(BF16)
    z = _dot(lr, wgu_ref[...]) + bg_ref[...]
    la_ref[...] = (jnp.minimum(z, 0.0) - jnp.log1p(jnp.exp(-jnp.abs(z)))) / GLA_GATE_NORM

    row = lax.broadcasted_iota(jnp.int32, (C, C), 0)
    col = lax.broadcasted_iota(jnp.int32, (C, C), 1)
    causal = row >= col
    tril = causal.astype(BF16)
    nw = nw_ref[...]

    nc = rows // C
    hs = [slice(h * LANES, (h + 1) * LANES) for h in range(GLA_HEADS)]
    rs = [slice(c * C, (c + 1) * C) for c in range(nc)]

    q_dec, k_inv, k_end, decay, vv = [], [], [], [], []
    for c in range(nc):
        la = la_ref[rs[c], :]
        hi = la.astype(BF16)
        r1 = la - hi.astype(F32)
        mid = r1.astype(BF16)
        lo = (r1 - mid.astype(F32)).astype(BF16)
        bcum = _dot(tril, hi) + _dot(tril, mid) + _dot(tril, lo)
        b_end = bcum[C - 1:C, :]
        q = q_ref[rs[c], :].astype(F32) * (GLA_DK ** -0.5)
        k = k_ref[rs[c], :].astype(F32)
        q_dec.append((q * jnp.exp(bcum)).astype(BF16))
        k_inv.append((k * jnp.exp(-bcum)).astype(BF16))
        k_end.append((k * jnp.exp(b_end - bcum)).astype(BF16))
        decay.append(jnp.exp(b_end))
        vv.append(v_ref[rs[c], :].astype(BF16))

    attn = [[jnp.where(causal, _dot_nt(q_dec[c][:, sl], k_inv[c][:, sl]), 0.0).astype(BF16)
             for sl in hs] for c in range(nc)]
    kv = [[_dot_tn(vv[c][:, sl], k_end[c][:, sl]) for sl in hs] for c in range(nc)]
    o_intra = [[_dot(attn[c][h], vv[c][:, hs[h]]) for h in range(GLA_HEADS)] for c in range(nc)]

    st = [st_ref[h] for h in range(GLA_HEADS)]
    for c in range(nc):
        g = g_ref[rs[c], :].astype(F32)
        for h, sl in enumerate(hs):
            o = o_intra[c][h] + _dot_nt(q_dec[c][:, sl], st[h].astype(BF16))
            st[h] = decay[c][:, sl] * st[h] + kv[c][h]
            o = o * lax.rsqrt(jnp.mean(o * o, axis=-1, keepdims=True) + RMS_EPS) * nw
            gh = g[:, sl]
            o = o * (gh * jax.nn.sigmoid(gh))
            o_ref[rs[c], sl] = o.astype(o_ref.dtype)
    for h in range(GLA_HEADS):
        st_ref[h] = st[h]


def _gla(xb, proj, w_lr, w_gu, b_gate, norm_w, *, batch, rows):
    t = xb.shape[0]
    steps = t // batch // rows
    w = GLA_HEADS * GLA_DK

    def col(off):
        return pl.BlockSpec((rows, w), lambda b, s: (b * steps + s, off // w))

    def full(shape):
        return pl.BlockSpec(shape, lambda b, s: (0,) * len(shape))

    return pl.pallas_call(
        _gla_kernel,
        grid=(batch, steps),
        in_specs=[pl.BlockSpec((rows, D_MODEL), lambda b, s: (b * steps + s, 0)),
                  col(COL_GQ), col(COL_GK), col(COL_GV), col(COL_GG),
                  full((D_MODEL, LANES)), full((LANES, w)), full((1, w)), full((1, GLA_DV))],
        out_specs=pl.BlockSpec((rows, w), lambda b, s: (b * steps + s, 0)),
        out_shape=jax.ShapeDtypeStruct((t, w), BF16),
        scratch_shapes=[pltpu.VMEM((GLA_HEADS, GLA_DV, GLA_DK), F32),
                        pltpu.VMEM((rows, w), F32)],
        compiler_params=_params("arbitrary", "arbitrary"),
        name="gla",
    )(xb, proj, proj, proj, proj, w_lr, w_gu, b_gate, norm_w)


def _gmlp_kernel(z_ref, lng_ref, lnb_ref, ws_ref, bs_ref, o_ref):
    rows = z_ref.shape[0]
    C = GMLP_CHUNK
    z = z_ref[...].astype(F32)
    z = 0.5 * z * (1.0 + lax.erf(z * np.float32(np.sqrt(0.5))))
    u = z[:, :GMLP_WIDTH]
    v = _layer_norm(z[:, GMLP_WIDTH:], lng_ref[...], lnb_ref[...]).astype(BF16)
    row = lax.broadcasted_iota(jnp.int32, (C, C), 0)
    col = lax.broadcasted_iota(jnp.int32, (C, C), 1)
    tril = (row >= col).astype(F32)
    for g in range(GMLP_GROUPS):
        sl = slice(g * GMLP_GROUP_CH, (g + 1) * GMLP_GROUP_CH)
        w = (ws_ref[g] * tril).astype(BF16)
        for c in range(rows // C):
            rs = slice(c * C, (c + 1) * C)
            vs = _dot(w, v[rs, sl]) + bs_ref[:, sl]
            o_ref[rs, sl] = (u[rs, sl] * vs).astype(o_ref.dtype)


def _gmlp(proj, ln_g, ln_b, w_s, bs_full, *, rows):
    t = proj.shape[0]

    def full(shape):
        return pl.BlockSpec(shape, lambda i: (0,) * len(shape))

    return pl.pallas_call(
        _gmlp_kernel,
        grid=(t // rows,),
        in_specs=[pl.BlockSpec((rows, 2 * GMLP_WIDTH), lambda i: (i, COL_GZ // (2 * GMLP_WIDTH))),
                  full((1, GMLP_WIDTH)), full((1, GMLP_WIDTH)),
                  full((GMLP_GROUPS, GMLP_CHUNK, GMLP_CHUNK)), full((GMLP_CHUNK, GMLP_WIDTH))],
        out_specs=pl.BlockSpec((rows, GMLP_WIDTH), lambda i: (i, 0)),
        out_shape=jax.ShapeDtypeStruct((t, GMLP_WIDTH), BF16),
        compiler_params=_params("parallel"),
        name="gmlp",
    )(proj, ln_g, ln_b, w_s, bs_full)


def _rope_table_kernel(pos_ref, inv_ref, c_ref, s1_ref, s2_ref):
    half = ROPE_DIMS // 2
    ang = pos_ref[...].astype(F32) * inv_ref[...]
    lane = lax.broadcasted_iota(jnp.int32, ang.shape, 1)
    cos = jnp.cos(ang)
    sin = jnp.sin(ang)
    c_ref[...] = jnp.where(lane < ROPE_DIMS, cos, 1.0)
    s1_ref[...] = jnp.where(lane < half, -sin, 0.0)
    s2_ref[...] = jnp.where((lane >= half) & (lane < ROPE_DIMS), sin, 0.0)


def _rope_tables(pos_col, inv_row, *, rows):
    t = pos_col.shape[0]
    spec = pl.BlockSpec((rows, LANES), lambda i: (i, 0))
    shp = jax.ShapeDtypeStruct((t, LANES), F32)
    return pl.pallas_call(
        _rope_table_kernel,
        grid=(t // rows,),
        in_specs=[pl.BlockSpec((rows, 1), lambda i: (i, 0)),
                  pl.BlockSpec((1, LANES), lambda i: (0, 0))],
        out_specs=[spec, spec, spec],
        out_shape=[shp, shp, shp],
        compiler_params=_params("parallel"),
        name="rope_tables",
    )(pos_col, inv_row)


def _moba_prep_kernel(q_ref, k_ref, v_ref, c_ref, s1_ref, s2_ref, qt_ref, ko_ref, vt_ref, km_ref):
    half = ROPE_DIMS // 2
    cos = c_ref[...]
    s1 = s1_ref[...]
    s2 = s2_ref[...]
    q_scale = np.float32(MOBA_HD ** -0.5 * np.log2(np.e))

    def rope(t):
        return (t * cos + pltpu.roll(t, LANES - half, 1) * s1 + pltpu.roll(t, half, 1) * s2)

    for h in range(MOBA_HEADS):
        sl = slice(h * LANES, (h + 1) * LANES)
        qr = rope(q_ref[:, sl].astype(F32)) * q_scale
        qt_ref[sl, :] = qr.T.astype(qt_ref.dtype)
        kr = rope(k_ref[:, sl].astype(F32))
        ko_ref[:, sl] = kr.astype(ko_ref.dtype)
        km_ref[:, sl] = jnp.mean(kr, axis=0, keepdims=True)
        vt_ref[h * MOBA_VT_ROWS:h * MOBA_VT_ROWS + MOBA_HD, :] = (
            v_ref[:, sl].astype(F32).T.astype(vt_ref.dtype))
        vt_ref[h * MOBA_VT_ROWS + MOBA_HD:(h + 1) * MOBA_VT_ROWS, :] = jnp.ones(
            (MOBA_VT_ROWS - MOBA_HD, MOBA_BLOCK), vt_ref.dtype)


def _moba_prep(proj3, cos3, s13, s23):
    batch, seq, _ = proj3.shape
    w = MOBA_HEADS * MOBA_HD
    nb = seq // MOBA_BLOCK
    tab = pl.BlockSpec((None, MOBA_BLOCK, LANES), lambda b, i: (b, i, 0))

    def col(off):
        return pl.BlockSpec((None, MOBA_BLOCK, w), lambda b, i: (b, i, off // w))

    vw = MOBA_HEADS * MOBA_VT_ROWS
    return pl.pallas_call(
        _moba_prep_kernel,
        grid=(batch, nb),
        in_specs=[col(COL_MQ), col(COL_MK), col(COL_MV), tab, tab, tab],
        out_specs=[pl.BlockSpec((None, w, MOBA_BLOCK), lambda b, i: (b, 0, i)),
                   pl.BlockSpec((None, MOBA_BLOCK, w), lambda b, i: (b, i, 0)),
                   pl.BlockSpec((None, vw, MOBA_BLOCK), lambda b, i: (b, 0, i)),
                   pl.BlockSpec((None, None, 1, w), lambda b, i: (b, i, 0, 0))],
        out_shape=[jax.ShapeDtypeStruct((batch, w, seq), BF16),
                   jax.ShapeDtypeStruct((batch, seq, w), BF16),
                   jax.ShapeDtypeStruct((batch, vw, seq), BF16),
                   jax.ShapeDtypeStruct((batch, nb, 1, w), F32)],
        compiler_params=_params("parallel", "parallel"),
        name="moba_prep",
    )(proj3, proj3, proj3, cos3, s13, s23)


def _moba_flash_kernel(qt_ref, k_ref, vt_ref, km_ref, o_ref, bias_ref, s_ref, acc_ref, *, heads):
    BLK = MOBA_BLOCK
    qi = pl.program_id(2)
    n_blk = km_ref.shape[1]
    streams = [(b, h) for b in range(qt_ref.shape[0]) for h in range(heads)]
    r0 = pl.multiple_of(qi * BLK, BLK)
    blk = lax.broadcasted_iota(jnp.int32, (n_blk, BLK), 0)
    blk_f = blk.astype(F32)
    key = lax.broadcasted_iota(jnp.int32, (BLK, BLK), 0)
    qry = lax.broadcasted_iota(jnp.int32, (BLK, BLK), 1)
    hs = [slice(h * MOBA_HD, (h + 1) * MOBA_HD) for h in range(heads)]
    vs = [slice(h * MOBA_VT_ROWS, (h + 1) * MOBA_VT_ROWS) for h in range(heads)]

    carry0 = []
    for n, (b, h) in enumerate(streams):
        qt = qt_ref[b, hs[h], :]
        km = km_ref[b, :, hs[h]]
        km_hi = km.astype(BF16)
        km_lo = (km - km_hi.astype(F32)).astype(BF16)
        gate = _dot(km_hi, qt) + _dot(km_lo, qt)
        cand = blk < qi
        sel = jnp.zeros(gate.shape, jnp.bool_)
        for r in range(min(MOBA_TOPK, n_blk)):
            g_eff = jnp.where(cand, gate, -jnp.inf)
            best = jnp.max(g_eff, axis=0, keepdims=True)
            idx = jnp.min(jnp.where(cand & (g_eff == best), blk_f, float(n_blk)),
                          axis=0, keepdims=True)
            idx = jnp.where(r < qi, idx, float(n_blk))
            sel = sel | (blk_f == idx)
            cand = cand & (blk_f != idx)
        bias_ref[n] = jnp.where(sel, 0.0, -jnp.inf)
        s0 = _dot(k_ref[b, pl.ds(0, BLK), hs[h]], qt)
        s_ref[n] = s0
        acc_ref[n] = jnp.zeros((MOBA_VT_ROWS, BLK), F32)
        carry0 += [jnp.full((1, BLK), -jnp.inf, F32), jnp.max(s0, axis=0, keepdims=True)]

    def body(j, carry):
        c0 = pl.multiple_of(j * BLK, BLK)
        n0 = pl.multiple_of((j + 1) * BLK, BLK)
        out = []
        for n, (b, h) in enumerate(streams):
            m, s_max = carry[2 * n:2 * n + 2]
            s_next = _dot(k_ref[b, pl.ds(n0, BLK), hs[h]], qt_ref[b, hs[h], :])
            bias = bias_ref[n, pl.ds(j, 1), :]
            m_new = jnp.maximum(m, s_max + bias)
            m_use = jnp.where(m_new == -jnp.inf, 0.0, m_new)
            alpha = jnp.exp2(m - m_use)
            p = jnp.exp2(s_ref[n] - (m_use - bias))
            acc_ref[n] = alpha * acc_ref[n] + _dot(vt_ref[b, vs[h], pl.ds(c0, BLK)],
                                                   p.astype(BF16))
            s_ref[n] = s_next
            out += [m_new, jnp.max(s_next, axis=0, keepdims=True)]
        return tuple(out)

    carry = lax.fori_loop(0, qi, body, tuple(carry0))
    for n, (b, h) in enumerate(streams):
        m = carry[2 * n]
        s = jnp.where(key <= qry, s_ref[n], -jnp.inf)
        m_new = jnp.maximum(m, jnp.max(s, axis=0, keepdims=True))
        alpha = jnp.exp2(m - m_new)
        p = jnp.exp2(s - m_new)
        acc = alpha * acc_ref[n] + _dot(vt_ref[b, vs[h], pl.ds(r0, BLK)], p.astype(BF16))
        o_ref[b, :, hs[h]] = (acc[:MOBA_HD] / acc[MOBA_HD:MOBA_HD + 1]).T.astype(o_ref.dtype)


def _moba_flash(q_t, k_rot, v_t, k_mean, *, rows, heads):
    batch, seq, w = k_rot.shape
    n_blk = seq // MOBA_BLOCK
    hw = heads * MOBA_HD
    n_streams = rows * heads
    once = pl.Buffered(1)
    return pl.pallas_call(
        functools.partial(_moba_flash_kernel, heads=heads),
        grid=(batch // rows, MOBA_HEADS // heads, n_blk),
        in_specs=[pl.BlockSpec((rows, hw, MOBA_BLOCK), lambda b, g, i: (b, g, i)),
                  pl.BlockSpec((rows, seq, hw), lambda b, g, i: (b, 0, g), pipeline_mode=once),
                  pl.BlockSpec((rows, heads * MOBA_VT_ROWS, seq), lambda b, g, i: (b, g, 0),
                               pipeline_mode=once),
                  pl.BlockSpec((rows, n_blk, hw), lambda b, g, i: (b, 0, g))],
        out_specs=pl.BlockSpec((rows, MOBA_BLOCK, hw), lambda b, g, i: (b, i, g)),
        out_shape=jax.ShapeDtypeStruct((batch, seq, w), BF16),
        scratch_shapes=[pltpu.VMEM((n_streams, n_blk, MOBA_BLOCK), F32),
                        pltpu.VMEM((n_streams, MOBA_BLOCK, MOBA_BLOCK), F32),
                        pltpu.VMEM((n_streams, MOBA_VT_ROWS, MOBA_BLOCK), F32)],
        compiler_params=_params("parallel", "parallel", "arbitrary"),
        name="moba_flash",
    )(q_t, k_rot, v_t, k_mean)


def _merge_kernel(a_ref, b_ref, c_ref, gl_ref, x_ref, wb_ref, wo_ref, g_ref, bt_ref,
                  xo_ref, xb_ref):
    gl = gl_ref[...].astype(F32)
    d = D_MODEL
    m = (jax.nn.sigmoid(gl[:, :d]) * _dot(a_ref[...], wb_ref[0])
         + jax.nn.sigmoid(gl[:, d:2 * d]) * _dot(b_ref[...], wb_ref[1])
         + jax.nn.sigmoid(gl[:, 2 * d:]) * _dot(c_ref[...], wb_ref[2]))
    mix = _dot(m.astype(BF16), wo_ref[...])
    y = _layer_norm(DEEPNORM_ALPHA * x_ref[...] + mix, g_ref[...], bt_ref[...])
    xo_ref[...] = y
    xb_ref[...] = y.astype(BF16)


def _merge(a, b, c, proj, x, w_branch, w_out, ln_g, ln_b, *, tm):
    t = x.shape[0]
    br = pl.BlockSpec((tm, BRANCH_W), lambda i: (i, 0))
    xs = pl.BlockSpec((tm, D_MODEL), lambda i: (i, 0))

    def full(shape):
        return pl.BlockSpec(shape, lambda i: (0,) * len(shape))

    return pl.pallas_call(
        _merge_kernel,
        grid=(t // tm,),
        in_specs=[br, br, br,
                  pl.BlockSpec((tm, 3 * D_MODEL), lambda i: (i, COL_GL // (3 * D_MODEL))),
                  xs, full((3, BRANCH_W, D_MODEL)), full((D_MODEL, D_MODEL)),
                  full((1, D_MODEL)), full((1, D_MODEL))],
        out_specs=[xs, xs],
        out_shape=[jax.ShapeDtypeStruct((t, D_MODEL), F32),
                   jax.ShapeDtypeStruct((t, D_MODEL), BF16)],
        compiler_params=_params("parallel"),
        name="merge_ln1",
    )(a, b, c, proj, x, w_branch, w_out, ln_g, ln_b)


def _ffn_kernel(xb_ref, x_ref, w1_ref, w2_ref, g_ref, bt_ref, xo_ref, xbo_ref, acc_ref):
    f = pl.program_id(1)

    @pl.when(f == 0)
    def _():
        acc_ref[...] = jnp.zeros_like(acc_ref)

    h = jnp.square(jnp.maximum(_dot(xb_ref[...], w1_ref[...]), 0.0)).astype(BF16)
    acc_ref[...] += _dot(h, w2_ref[...])

    @pl.when(f == pl.num_programs(1) - 1)
    def _():
        y = _layer_norm(DEEPNORM_ALPHA * x_ref[...] + acc_ref[...], g_ref[...], bt_ref[...])
        xo_ref[...] = y
        xbo_ref[...] = y.astype(BF16)


def _ffn(xb, x, w1, w2, ln_g, ln_b, *, tm, tf):
    t = x.shape[0]
    xs = pl.BlockSpec((tm, D_MODEL), lambda i, f: (i, 0))
    vec = pl.BlockSpec((1, D_MODEL), lambda i, f: (0, 0))
    return pl.pallas_call(
        _ffn_kernel,
        grid=(t // tm, D_FF // tf),
        in_specs=[xs, xs,
                  pl.BlockSpec((D_MODEL, tf), lambda i, f: (0, f)),
                  pl.BlockSpec((tf, D_MODEL), lambda i, f: (f, 0)),
                  vec, vec],
        out_specs=[xs, xs],
        out_shape=[jax.ShapeDtypeStruct((t, D_MODEL), F32),
                   jax.ShapeDtypeStruct((t, D_MODEL), BF16)],
        scratch_shapes=[pltpu.VMEM((tm, D_MODEL), F32)],
        compiler_params=_params("parallel", "arbitrary"),
        name="ffn_ln2",
    )(xb, x, w1, w2, ln_g, ln_b)


def _split_w_in(w_in_l):
    sizes = [512, 512, 512, 512, GLA_GATE_RANK, 1024, 512, 512, 512, 3 * D_MODEL]
    offs = np.concatenate([[0], np.cumsum(sizes)])
    gq, gk, gv, gg, glr, gz, mq, mk, mv, gl = [w_in_l[:, offs[i]:offs[i + 1]] for i in range(10)]
    w_main = jnp.concatenate([gl, gq, gk, gv, gg, gz, mq, mk, mv], axis=1).astype(BF16)
    w_lr = jnp.pad(glr, ((0, 0), (0, LANES - GLA_GATE_RANK))).astype(BF16)
    return w_main, w_lr


def kernel(x, positions, w_in, w_gate_up, b_gate, gla_norm_w, gmlp_ln_g, gmlp_ln_b, gmlp_w_s,
           gmlp_b_s, w_branch, w_out, ln1_g, ln1_b, w_ff1, w_ff2, ln2_g, ln2_b):
    batch, seq, d = x.shape
    t = batch * seq
    depth = w_in.shape[0]
    assert d == D_MODEL and seq % 512 == 0

    half = ROPE_DIMS // 2
    inv = 1.0 / (ROPE_THETA ** (jnp.arange(half, dtype=F32) * (2.0 / ROPE_DIMS)))
    inv_row = jnp.concatenate([inv, inv, jnp.zeros((LANES - ROPE_DIMS,), F32)])[None, :]
    cos, s1, s2 = [a.reshape(batch, seq, LANES) for a in
                   _rope_tables(positions.reshape(t, 1), inv_row, rows=min(1024, seq))]

    xf = x.reshape(t, d)
    xb = xf.astype(BF16)
    for l in range(depth):
        w_main, w_lr = _split_w_in(w_in[l])
        w_gu = jnp.pad(w_gate_up[l], ((0, LANES - GLA_GATE_RANK), (0, 0))).astype(BF16)
        bs_full = jnp.repeat(gmlp_b_s[l].T, GMLP_GROUP_CH, axis=1)

        proj = _matmul(xb, w_main, tm=1024, tn=1536, out_dtype=BF16)
        a = _gla(xb, proj, w_lr, w_gu, b_gate[l][None, :], gla_norm_w[l][None, :],
                 batch=batch, rows=512)
        b = _gmlp(proj, gmlp_ln_g[l][None, :], gmlp_ln_b[l][None, :], gmlp_w_s[l], bs_full,
                  rows=512)
        q_t, k_rot, v_t, k_mean = _moba_prep(proj.reshape(batch, seq, -1), cos, s1, s2)
        c = _moba_flash(q_t, k_rot, v_t, k_mean.reshape(batch, seq // MOBA_BLOCK, -1),
                        rows=2 if batch % 2 == 0 else 1, heads=MOBA_HEADS)
        xf, xb = _merge(a, b, c.reshape(t, -1), proj, xf, w_branch[l].astype(BF16),
                        w_out[l].astype(BF16), ln1_g[l][None, :], ln1_b[l][None, :], tm=512)
        xf, xb = _ffn(xb, xf, w_ff1[l].astype(BF16), w_ff2[l].astype(BF16),
                      ln2_g[l][None, :], ln2_b[l][None, :], tm=1024, tf=1024)
    return xf.reshape(batch, seq, d)
```

```python
import functools

import numpy as np
import jax
import jax.numpy as jnp
from jax import lax
from jax.experimental import pallas as pl
from jax.experimental.pallas import tpu as pltpu

D_MODEL = 1024
DEPTH = 4
GLA_HEADS = 4
GLA_DK = 128
GLA_DV = 128
GLA_GATE_RANK = 16
GLA_GATE_NORM = 16.0
GLA_CHUNK = 64
GMLP_GROUPS = 4
GMLP_GROUP_CH = 128
GMLP_WIDTH = 512
GMLP_CHUNK = 128
MOBA_HEADS = 4
MOBA_HD = 128
MOBA_BLOCK = 256
MOBA_TOPK = 3
MOBA_VT_ROWS = MOBA_HD + 16
ROPE_THETA = 500000.0
ROPE_DIMS = MOBA_HD // 4
BRANCH_W = 512
D_FF = 4 * D_MODEL
DEEPNORM_ALPHA = (2 * DEPTH) ** 0.25
LN_EPS = 1e-5
RMS_EPS = 1e-6

LANES = 128
VMEM_LIMIT = 56 * 1024 * 1024

COL_GL = 0
COL_GQ = 3072
COL_GK = 3584
COL_GV = 4096
COL_GG = 4608
COL_GZ = 5120
COL_MQ = 6144
COL_MK = 6656
COL_MV = 7168
PROJ_COLS = 7680

BF16 = jnp.bfloat16
F32 = jnp.float32


def _dot(a, b):
    return jnp.dot(a, b, preferred_element_type=F32)


def _dot_nt(a, b):
    return lax.dot_general(a, b, (((1,), (1,)), ((), ())), preferred_element_type=F32)


def _dot_tn(a, b):
    return lax.dot_general(a, b, (((0,), (0,)), ((), ())), preferred_element_type=F32)


def _params(*sem):
    return pltpu.CompilerParams(dimension_semantics=sem, vmem_limit_bytes=VMEM_LIMIT)


def _layer_norm(y, g, b):
    mu = jnp.mean(y, axis=-1, keepdims=True)
    yc = y - mu
    var = jnp.mean(yc * yc, axis=-1, keepdims=True)
    return yc * lax.rsqrt(var + LN_EPS) * g + b


def _matmul_kernel(x_ref, w_ref, o_ref):
    o_ref[...] = _dot(x_ref[...], w_ref[...]).astype(o_ref.dtype)


def _matmul(x, w, layer, *, tm, tn, out_dtype):
    m, k = x.shape
    n = w.shape[2]
    return pl.pallas_call(
        _matmul_kernel,
        grid=(m // tm, n // tn),
        in_specs=[pl.BlockSpec((tm, k), lambda i, j: (i, 0)),
                  pl.BlockSpec((None, k, tn), lambda i, j: (layer, 0, j))],
        out_specs=pl.BlockSpec((tm, tn), lambda i, j: (i, j)),
        out_shape=jax.ShapeDtypeStruct((m, n), out_dtype),
        compiler_params=_params("parallel", "parallel"),
        name="in_proj",
    )(x, w)


def _gla_kernel(x_ref, q_ref, k_ref, v_ref, g_ref, wlr_ref, wgu_ref, bg_ref, nw_ref,
                o_ref, st_ref, la_ref):
    rows = x_ref.shape[0]
    C = GLA_CHUNK

    @pl.when(pl.program_id(1) == 0)
    def _():
        st_ref[...] = jnp.zeros_like(st_ref)

    lr = _dot(x_ref[...], wlr_ref[...]).astype(BF16)
    z = _dot(lr, wgu_ref[...]) + bg_ref[...]
    la_ref[...] = (jnp.minimum(z, 0.0) - jnp.log1p(jnp.exp(-jnp.abs(z)))) / GLA_GATE_NORM

    row = lax.broadcasted_iota(jnp.int32, (C, C), 0)
    col = lax.broadcasted_iota(jnp.int32, (C, C), 1)
    causal = row >= col
    tril = causal.astype(BF16)
    nw = nw_ref[...]

    nc = rows // C
    hs = [slice(h * LANES, (h + 1) * LANES) for h in range(GLA_HEADS)]
    rs = [slice(c * C, (c + 1) * C) for c in range(nc)]

    q_dec, k_inv, k_end, decay, vv = [], [], [], [], []
    for c in range(nc):
        la = la_ref[rs[c], :]
        hi = la.astype(BF16)
        r1 = la - hi.astype(F32)
        mid = r1.astype(BF16)
        lo = (r1 - mid.astype(F32)).astype(BF16)
        bcum = _dot(tril, hi) + _dot(tril, mid) + _dot(tril, lo)
        b_end = bcum[C - 1:C, :]
        q = q_ref[rs[c], :].astype(F32) * (GLA_DK ** -0.5)
        k = k_ref[rs[c], :].astype(F32)
        q_dec.append((q * jnp.exp(bcum)).astype(BF16))
        k_inv.append((k * jnp.exp(-bcum)).astype(BF16))
        k_end.append((k * jnp.exp(b_end - bcum)).astype(BF16))
        decay.append(jnp.exp(b_end))
        vv.append(v_ref[rs[c], :].astype(BF16))

    attn = [[jnp.where(causal, _dot_nt(q_dec[c][:, sl], k_inv[c][:, sl]), 0.0).astype(BF16)
             for sl in hs] for c in range(nc)]
    kv = [[_dot_tn(vv[c][:, sl], k_end[c][:, sl]) for sl in hs] for c in range(nc)]
    o_intra = [[_dot(attn[c][h], vv[c][:, hs[h]]) for h in range(GLA_HEADS)] for c in range(nc)]

    st = [st_ref[h] for h in range(GLA_HEADS)]
    for c in range(nc):
        g = g_ref[rs[c], :].astype(F32)
        for h, sl in enumerate(hs):
            o = o_intra[c][h] + _dot_nt(q_dec[c][:, sl], st[h].astype(BF16))
            st[h] = decay[c][:, sl] * st[h] + kv[c][h]
            o = o * lax.rsqrt(jnp.mean(o * o, axis=-1, keepdims=True) + RMS_EPS) * nw
            gh = g[:, sl]
            o = o * (gh * jax.nn.sigmoid(gh))
            o_ref[rs[c], sl] = o.astype(o_ref.dtype)
    for h in range(GLA_HEADS):
        st_ref[h] = st[h]


def _gla(xb, proj, w_lr, w_gu, b_gate, norm_w, layer, *, batch, rows):
    t = xb.shape[0]
    steps = t // batch // rows
    w = GLA_HEADS * GLA_DK

    def col(off):
        return pl.BlockSpec((rows, w), lambda b, s: (b * steps + s, off // w))

    def full(shape):
        return pl.BlockSpec((None,) + shape, lambda b, s: (layer,) + (0,) * len(shape))

    return pl.pallas_call(
        _gla_kernel,
        grid=(batch, steps),
        in_specs=[pl.BlockSpec((rows, D_MODEL), lambda b, s: (b * steps + s, 0)),
                  col(COL_GQ), col(COL_GK), col(COL_GV), col(COL_GG),
                  full((D_MODEL, LANES)), full((LANES, w)), full((1, w)), full((1, GLA_DV))],
        out_specs=pl.BlockSpec((rows, w), lambda b, s: (b * steps + s, 0)),
        out_shape=jax.ShapeDtypeStruct((t, w), BF16),
        scratch_shapes=[pltpu.VMEM((GLA_HEADS, GLA_DV, GLA_DK), F32),
                        pltpu.VMEM((rows, w), F32)],
        compiler_params=_params("arbitrary", "arbitrary"),
        name="gla",
    )(xb, proj, proj, proj, proj, w_lr, w_gu, b_gate, norm_w)


def _gmlp_kernel(z_ref, lng_ref, lnb_ref, ws_ref, bs_ref, o_ref):
    rows = z_ref.shape[0]
    C = GMLP_CHUNK
    z = z_ref[...].astype(F32)
    z = 0.5 * z * (1.0 + lax.erf(z * np.float32(np.sqrt(0.5))))
    u = z[:, :GMLP_WIDTH]
    v = _layer_norm(z[:, GMLP_WIDTH:], lng_ref[...], lnb_ref[...]).astype(BF16)
    row = lax.broadcasted_iota(jnp.int32, (C, C), 0)
    col = lax.broadcasted_iota(jnp.int32, (C, C), 1)
    tril = (row >= col).astype(F32)
    for g in range(GMLP_GROUPS):
        sl = slice(g * GMLP_GROUP_CH, (g + 1) * GMLP_GROUP_CH)
        w = (ws_ref[g] * tril).astype(BF16)
        for c in range(rows // C):
            rs = slice(c * C, (c + 1) * C)
            vs = _dot(w, v[rs, sl]) + bs_ref[:, sl]
            o_ref[rs, sl] = (u[rs, sl] * vs).astype(o_ref.dtype)


def _gmlp(proj, ln_g, ln_b, w_s, bs_full, layer, *, rows):
    t = proj.shape[0]

    def full(shape):
        return pl.BlockSpec((None,) + shape, lambda i: (layer,) + (0,) * len(shape))

    return pl.pallas_call(
        _gmlp_kernel,
        grid=(t // rows,),
        in_specs=[pl.BlockSpec((rows, 2 * GMLP_WIDTH), lambda i: (i, COL_GZ // (2 * GMLP_WIDTH))),
                  full((1, GMLP_WIDTH)), full((1, GMLP_WIDTH)),
                  full((GMLP_GROUPS, GMLP_CHUNK, GMLP_CHUNK)), full((GMLP_CHUNK, GMLP_WIDTH))],
        out_specs=pl.BlockSpec((rows, GMLP_WIDTH), lambda i: (i, 0)),
        out_shape=jax.ShapeDtypeStruct((t, GMLP_WIDTH), BF16),
        compiler_params=_params("parallel"),
        name="gmlp",
    )(proj, ln_g, ln_b, w_s, bs_full)


def _rope_table_kernel(pos_ref, inv_ref, c_ref, s1_ref, s2_ref):
    half = ROPE_DIMS // 2
    ang = pos_ref[...].astype(F32) * inv_ref[...]
    lane = lax.broadcasted_iota(jnp.int32, ang.shape, 1)
    cos = jnp.cos(ang)
    sin = jnp.sin(ang)
    c_ref[...] = jnp.where(lane < ROPE_DIMS, cos, 1.0)
    s1_ref[...] = jnp.where(lane < half, -sin, 0.0)
    s2_ref[...] = jnp.where((lane >= half) & (lane < ROPE_DIMS), sin, 0.0)


def _rope_tables(pos_col, inv_row, *, rows):
    t = pos_col.shape[0]
    spec = pl.BlockSpec((rows, LANES), lambda i: (i, 0))
    shp = jax.ShapeDtypeStruct((t, LANES), F32)
    return pl.pallas_call(
        _rope_table_kernel,
        grid=(t // rows,),
        in_specs=[pl.BlockSpec((rows, 1), lambda i: (i, 0)),
                  pl.BlockSpec((1, LANES), lambda i: (0, 0))],
        out_specs=[spec, spec, spec],
        out_shape=[shp, shp, shp],
        compiler_params=_params("parallel"),
        name="rope_tables",
    )(pos_col, inv_row)


def _moba_prep_kernel(q_ref, k_ref, v_ref, c_ref, s1_ref, s2_ref, qt_ref, ko_ref, vt_ref, km_ref):
    half = ROPE_DIMS // 2
    q_scale = np.float32(MOBA_HD ** -0.5 * np.log2(np.e))

    for n in range(q_ref.shape[0] // MOBA_BLOCK):
        rs = slice(n * MOBA_BLOCK, (n + 1) * MOBA_BLOCK)
        cos = c_ref[rs, :]
        s1 = s1_ref[rs, :]
        s2 = s2_ref[rs, :]

        def rope(t):
            return (t * cos + pltpu.roll(t, LANES - half, 1) * s1 + pltpu.roll(t, half, 1) * s2)

        for h in range(MOBA_HEADS):
            sl = slice(h * LANES, (h + 1) * LANES)
            qr = rope(q_ref[rs, sl].astype(F32)) * q_scale
            qt_ref[sl, rs] = qr.T.astype(qt_ref.dtype)
            kr = rope(k_ref[rs, sl].astype(F32))
            ko_ref[rs, sl] = kr.astype(ko_ref.dtype)
            km_ref[n, :, sl] = jnp.mean(kr, axis=0, keepdims=True)
            vt_ref[h * MOBA_VT_ROWS:h * MOBA_VT_ROWS + MOBA_HD, rs] = (
                v_ref[rs, sl].astype(F32).T.astype(vt_ref.dtype))
            vt_ref[h * MOBA_VT_ROWS + MOBA_HD:(h + 1) * MOBA_VT_ROWS, rs] = jnp.ones(
                (MOBA_VT_ROWS - MOBA_HD, MOBA_BLOCK), vt_ref.dtype)


def _moba_prep(proj3, cos3, s13, s23, *, rows):
    batch, seq, _ = proj3.shape
    w = MOBA_HEADS * MOBA_HD
    nb = seq // MOBA_BLOCK
    tab = pl.BlockSpec((None, rows, LANES), lambda b, i: (b, i, 0))

    def col(off):
        return pl.BlockSpec((None, rows, w), lambda b, i: (b, i, off // w))

    vw = MOBA_HEADS * MOBA_VT_ROWS
    return pl.pallas_call(
        _moba_prep_kernel,
        grid=(batch, seq // rows),
        in_specs=[col(COL_MQ), col(COL_MK), col(COL_MV), tab, tab, tab],
        out_specs=[pl.BlockSpec((None, w, rows), lambda b, i: (b, 0, i)),
                   pl.BlockSpec((None, rows, w), lambda b, i: (b, i, 0)),
                   pl.BlockSpec((None, vw, rows), lambda b, i: (b, 0, i)),
                   pl.BlockSpec((None, rows // MOBA_BLOCK, 1, w), lambda b, i: (b, i, 0, 0))],
        out_shape=[jax.ShapeDtypeStruct((batch, w, seq), BF16),
                   jax.ShapeDtypeStruct((batch, seq, w), BF16),
                   jax.ShapeDtypeStruct((batch, vw, seq), BF16),
                   jax.ShapeDtypeStruct((batch, nb, 1, w), F32)],
        compiler_params=_params("parallel", "parallel"),
        name="moba_prep",
    )(proj3, proj3, proj3, cos3, s13, s23)


def _moba_flash_kernel(qt_ref, k_ref, vt_ref, km_ref, o_ref, bias_ref, s_ref, acc_ref, *, heads):
    BLK = MOBA_BLOCK
    qi = pl.program_id(2)
    n_blk = km_ref.shape[1]
    streams = [(b, h) for b in range(qt_ref.shape[0]) for h in range(heads)]
    r0 = pl.multiple_of(qi * BLK, BLK)
    blk = lax.broadcasted_iota(jnp.int32, (n_blk, BLK), 0)
    blk_f = blk.astype(F32)
    key = lax.broadcasted_iota(jnp.int32, (BLK, BLK), 0)
    qry = lax.broadcasted_iota(jnp.int32, (BLK, BLK), 1)
    hs = [slice(h * MOBA_HD, (h + 1) * MOBA_HD) for h in range(heads)]
    vs = [slice(h * MOBA_VT_ROWS, (h + 1) * MOBA_VT_ROWS) for h in range(heads)]

    carry0 = []
    for n, (b, h) in enumerate(streams):
        qt = qt_ref[b, hs[h], :]
        km = km_ref[b, :, hs[h]]
        km_hi = km.astype(BF16)
        km_lo = (km - km_hi.astype(F32)).astype(BF16)
        gate = _dot(km_hi, qt) + _dot(km_lo, qt)
        cand = blk < qi
        sel = jnp.zeros(gate.shape, jnp.bool_)
        for r in range(min(MOBA_TOPK, n_blk)):
            g_eff = jnp.where(cand, gate, -jnp.inf)
            best = jnp.max(g_eff, axis=0, keepdims=True)
            idx = jnp.min(jnp.where(cand & (g_eff == best), blk_f, float(n_blk)),
                          axis=0, keepdims=True)
            idx = jnp.where(r < qi, idx, float(n_blk))
            sel = sel | (blk_f == idx)
            cand = cand & (blk_f != idx)
        bias_ref[n] = jnp.where(sel, 0.0, -jnp.inf)
        s0 = _dot(k_ref[b, pl.ds(0, BLK), hs[h]], qt)
        s_ref[n] = s0
        acc_ref[n] = jnp.zeros((MOBA_VT_ROWS, BLK), F32)
        carry0 += [jnp.full((1, BLK), -jnp.inf, F32), jnp.max(s0, axis=0, keepdims=True)]

    def body(j, carry):
        c0 = pl.multiple_of(j * BLK, BLK)
        n0 = pl.multiple_of((j + 1) * BLK, BLK)
        out = []
        for n, (b, h) in enumerate(streams):
            m, s_max = carry[2 * n:2 * n + 2]
            s_next = _dot(k_ref[b, pl.ds(n0, BLK), hs[h]], qt_ref[b, hs[h], :])
            bias = bias_ref[n, pl.ds(j, 1), :]
            m_new = jnp.maximum(m, s_max + bias)
            m_use = jnp.where(m_new == -jnp.inf, 0.0, m_new)
            alpha = jnp.exp2(m - m_use)
            p = jnp.exp2(s_ref[n] - (m_use - bias))
            acc_ref[n] = alpha * acc_ref[n] + _dot(vt_ref[b, vs[h], pl.ds(c0, BLK)],
                                                   p.astype(BF16))
            s_ref[n] = s_next
            out += [m_new, jnp.max(s_next, axis=0, keepdims=True)]
        return tuple(out)

    carry = lax.fori_loop(0, qi, body, tuple(carry0))
    for n, (b, h) in enumerate(streams):
        m = carry[2 * n]
        s = jnp.where(key <= qry, s_ref[n], -jnp.inf)
        m_new = jnp.maximum(m, jnp.max(s, axis=0, keepdims=True))
        alpha = jnp.exp2(m - m_new)
        p = jnp.exp2(s - m_new)
        acc = alpha * acc_ref[n] + _dot(vt_ref[b, vs[h], pl.ds(r0, BLK)], p.astype(BF16))
        o_ref[b, :, hs[h]] = (acc[:MOBA_HD] / acc[MOBA_HD:MOBA_HD + 1]).T.astype(o_ref.dtype)


def _moba_flash(q_t, k_rot, v_t, k_mean, *, rows, heads):
    batch, seq, w = k_rot.shape
    n_blk = seq // MOBA_BLOCK
    hw = heads * MOBA_HD
    n_streams = rows * heads
    once = pl.Buffered(1)
    return pl.pallas_call(
        functools.partial(_moba_flash_kernel, heads=heads),
        grid=(batch // rows, MOBA_HEADS // heads, n_blk),
        in_specs=[pl.BlockSpec((rows, hw, MOBA_BLOCK), lambda b, g, i: (b, g, i)),
                  pl.BlockSpec((rows, seq, hw), lambda b, g, i: (b, 0, g), pipeline_mode=once),
                  pl.BlockSpec((rows, heads * MOBA_VT_ROWS, seq), lambda b, g, i: (b, g, 0),
                               pipeline_mode=once),
                  pl.BlockSpec((rows, n_blk, hw), lambda b, g, i: (b, 0, g))],
        out_specs=pl.BlockSpec((rows, MOBA_BLOCK, hw), lambda b, g, i: (b, i, g)),
        out_shape=jax.ShapeDtypeStruct((batch, seq, w), BF16),
        scratch_shapes=[pltpu.VMEM((n_streams, n_blk, MOBA_BLOCK), F32),
                        pltpu.VMEM((n_streams, MOBA_BLOCK, MOBA_BLOCK), F32),
                        pltpu.VMEM((n_streams, MOBA_VT_ROWS, MOBA_BLOCK), F32)],
        compiler_params=_params("parallel", "parallel", "arbitrary"),
        name="moba_flash",
    )(q_t, k_rot, v_t, k_mean)


def _merge_kernel(a_ref, b_ref, c_ref, gl_ref, x_ref, wb_ref, wo_ref, g_ref, bt_ref,
                  xo_ref, xb_ref):
    gl = gl_ref[...].astype(F32)
    d = D_MODEL
    m = (jax.nn.sigmoid(gl[:, :d]) * _dot(a_ref[...], wb_ref[0])
         + jax.nn.sigmoid(gl[:, d:2 * d]) * _dot(b_ref[...], wb_ref[1])
         + jax.nn.sigmoid(gl[:, 2 * d:]) * _dot(c_ref[...], wb_ref[2]))
    mix = _dot(m.astype(BF16), wo_ref[...])
    y = _layer_norm(DEEPNORM_ALPHA * x_ref[...] + mix, g_ref[...], bt_ref[...])
    xo_ref[...] = y
    xb_ref[...] = y.astype(BF16)


def _merge(a, b, c, proj, x, w_branch, w_out, ln_g, ln_b, layer, *, tm):
    t = x.shape[0]
    br = pl.BlockSpec((tm, BRANCH_W), lambda i: (i, 0))
    xs = pl.BlockSpec((tm, D_MODEL), lambda i: (i, 0))

    def full(shape):
        return pl.BlockSpec((None,) + shape, lambda i: (layer,) + (0,) * len(shape))

    return pl.pallas_call(
        _merge_kernel,
        grid=(t // tm,),
        in_specs=[br, br, br,
                  pl.BlockSpec((tm, 3 * D_MODEL), lambda i: (i, COL_GL // (3 * D_MODEL))),
                  xs, full((3, BRANCH_W, D_MODEL)), full((D_MODEL, D_MODEL)),
                  full((1, D_MODEL)), full((1, D_MODEL))],
        out_specs=[xs, xs],
        out_shape=[jax.ShapeDtypeStruct((t, D_MODEL), F32),
                   jax.ShapeDtypeStruct((t, D_MODEL), BF16)],
        compiler_params=_params("parallel"),
        name="merge_ln1",
    )(a, b, c, proj, x, w_branch, w_out, ln_g, ln_b)


def _ffn_kernel(xb_ref, x_ref, w1_ref, w2_ref, g_ref, bt_ref, xo_ref, xbo_ref, acc_ref):
    f = pl.program_id(1)

    @pl.when(f == 0)
    def _():
        acc_ref[...] = jnp.zeros_like(acc_ref)

    h = jnp.square(jnp.maximum(_dot(xb_ref[...], w1_ref[...]), 0.0)).astype(BF16)
    acc_ref[...] += _dot(h, w2_ref[...])

    @pl.when(f == pl.num_programs(1) - 1)
    def _():
        y = _layer_norm(DEEPNORM_ALPHA * x_ref[...] + acc_ref[...], g_ref[...], bt_ref[...])
        xo_ref[...] = y
        xbo_ref[...] = y.astype(BF16)


def _ffn(xb, x, w1, w2, ln_g, ln_b, layer, *, tm, tf):
    t = x.shape[0]
    xs = pl.BlockSpec((tm, D_MODEL), lambda i, f: (i, 0))
    vec = pl.BlockSpec((None, 1, D_MODEL), lambda i, f: (layer, 0, 0))
    return pl.pallas_call(
        _ffn_kernel,
        grid=(t // tm, D_FF // tf),
        in_specs=[xs, xs,
                  pl.BlockSpec((None, D_MODEL, tf), lambda i, f: (layer, 0, f)),
                  pl.BlockSpec((None, tf, D_MODEL), lambda i, f: (layer, f, 0)),
                  vec, vec],
        out_specs=[xs, xs],
        out_shape=[jax.ShapeDtypeStruct((t, D_MODEL), F32),
                   jax.ShapeDtypeStruct((t, D_MODEL), BF16)],
        scratch_shapes=[pltpu.VMEM((tm, D_MODEL), F32)],
        compiler_params=_params("parallel", "arbitrary"),
        name="ffn_ln2",
    )(xb, x, w1, w2, ln_g, ln_b)


def _split_w_in(w_in):
    sizes = [512, 512, 512, 512, GLA_GATE_RANK, 1024, 512, 512, 512, 3 * D_MODEL]
    offs = np.concatenate([[0], np.cumsum(sizes)])
    gq, gk, gv, gg, glr, gz, mq, mk, mv, gl = [w_in[..., offs[i]:offs[i + 1]] for i in range(10)]
    w_main = jnp.concatenate([gl, gq, gk, gv, gg, gz, mq, mk, mv], axis=-1).astype(BF16)
    w_lr = jnp.pad(glr, ((0, 0), (0, 0), (0, LANES - GLA_GATE_RANK))).astype(BF16)
    return w_main, w_lr


def kernel(x, positions, w_in, w_gate_up, b_gate, gla_norm_w, gmlp_ln_g, gmlp_ln_b, gmlp_w_s,
           gmlp_b_s, w_branch, w_out, ln1_g, ln1_b, w_ff1, w_ff2, ln2_g, ln2_b):
    batch, seq, d = x.shape
    t = batch * seq
    depth = w_in.shape[0]
    assert d == D_MODEL and seq % 512 == 0

    half = ROPE_DIMS // 2
    inv = 1.0 / (ROPE_THETA ** (jnp.arange(half, dtype=F32) * (2.0 / ROPE_DIMS)))
    inv_row = jnp.concatenate([inv, inv, jnp.zeros((LANES - ROPE_DIMS,), F32)])[None, :]
    cos, s1, s2 = [a.reshape(batch, seq, LANES) for a in
                   _rope_tables(positions.reshape(t, 1), inv_row, rows=min(1024, seq))]

    w_main, w_lr = _split_w_in(w_in)
    w_gu = jnp.pad(w_gate_up, ((0, 0), (0, LANES - GLA_GATE_RANK), (0, 0))).astype(BF16)
    bs_full = jnp.repeat(jnp.swapaxes(gmlp_b_s, 1, 2), GMLP_GROUP_CH, axis=2)
    w_branch_b, w_out_b = w_branch.astype(BF16), w_out.astype(BF16)
    w_ff1_b, w_ff2_b = w_ff1.astype(BF16), w_ff2.astype(BF16)
    row = lambda v: v[:, None, :]

    xf = x.reshape(t, d)
    xb = xf.astype(BF16)
    for l in range(depth):
        proj = _matmul(xb, w_main, l, tm=2048, tn=1536, out_dtype=BF16)
        a = _gla(xb, proj, w_lr, w_gu, row(b_gate), row(gla_norm_w), l, batch=batch, rows=512)
        b = _gmlp(proj, row(gmlp_ln_g), row(gmlp_ln_b), gmlp_w_s, bs_full, l, rows=512)
        q_t, k_rot, v_t, k_mean = _moba_prep(proj.reshape(batch, seq, -1), cos, s1, s2,
                                             rows=min(1024, seq))
        c = _moba_flash(q_t, k_rot, v_t, k_mean.reshape(batch, seq // MOBA_BLOCK, -1),
                        rows=2 if batch % 2 == 0 else 1, heads=MOBA_HEADS)
        xf, xb = _merge(a, b, c.reshape(t, -1), proj, xf, w_branch_b, w_out_b,
                        row(ln1_g), row(ln1_b), l, tm=512)
        xf, xb = _ffn(xb, xf, w_ff1_b, w_ff2_b, row(ln2_g), row(ln2_b), l, tm=1024, tf=1024)
    return xf.reshape(batch, seq, d)
```

```python
import functools

import numpy as np
import jax
import jax.numpy as jnp
from jax import lax
from jax.experimental import pallas as pl
from jax.experimental.pallas import tpu as pltpu

D_MODEL = 1024
DEPTH = 4
GLA_HEADS = 4
GLA_DK = 128
GLA_DV = 128
GLA_GATE_RANK = 16
GLA_GATE_NORM = 16.0
GLA_CHUNK = 64
GMLP_GROUPS = 4
GMLP_GROUP_CH = 128
GMLP_WIDTH = 512
GMLP_CHUNK = 128
MOBA_HEADS = 4
MOBA_HD = 128
MOBA_BLOCK = 256
MOBA_TOPK = 3
MOBA_VT_ROWS = MOBA_HD + 16
ROPE_THETA = 500000.0
ROPE_DIMS = MOBA_HD // 4
BRANCH_W = 512
D_FF = 4 * D_MODEL
DEEPNORM_ALPHA = (2 * DEPTH) ** 0.25
LN_EPS = 1e-5
RMS_EPS = 1e-6

LANES = 128
VMEM_LIMIT = 56 * 1024 * 1024

COL_GL = 0
COL_GQ = 3072
COL_GK = 3584
COL_GV = 4096
COL_GG = 4608
COL_GZ = 5120
COL_MQ = 6144
COL_MK = 6656
COL_MV = 7168
PROJ_COLS = 7680

BF16 = jnp.bfloat16
F32 = jnp.float32


def _dot(a, b):
    return jnp.dot(a, b, preferred_element_type=F32)


def _dot_nt(a, b):
    return lax.dot_general(a, b, (((1,), (1,)), ((), ())), preferred_element_type=F32)


def _dot_tn(a, b):
    return lax.dot_general(a, b, (((0,), (0,)), ((), ())), preferred_element_type=F32)


def _params(*sem):
    return pltpu.CompilerParams(dimension_semantics=sem, vmem_limit_bytes=VMEM_LIMIT)


def _layer_norm(y, g, b):
    mu = jnp.mean(y, axis=-1, keepdims=True)
    yc = y - mu
    var = jnp.mean(yc * yc, axis=-1, keepdims=True)
    return yc * lax.rsqrt(var + LN_EPS) * g + b


def _matmul_kernel(x_ref, w_ref, o_ref):
    o_ref[...] = _dot(x_ref[...], w_ref[...]).astype(o_ref.dtype)


def _matmul(x, w, layer, *, tm, tn, out_dtype):
    m, k = x.shape
    n = w.shape[2]
    return pl.pallas_call(
        _matmul_kernel,
        grid=(m // tm, n // tn),
        in_specs=[pl.BlockSpec((tm, k), lambda i, j: (i, 0)),
                  pl.BlockSpec((None, k, tn), lambda i, j: (layer, 0, j))],
        out_specs=pl.BlockSpec((tm, tn), lambda i, j: (i, j)),
        out_shape=jax.ShapeDtypeStruct((m, n), out_dtype),
        compiler_params=_params("parallel", "parallel"),
        name="in_proj",
    )(x, w)


def _gla_kernel(x_ref, q_ref, k_ref, v_ref, g_ref, wlr_ref, wgu_ref, bg_ref, nw_ref,
                o_ref, st_ref, la_ref):
    rows = x_ref.shape[0]
    C = GLA_CHUNK

    @pl.when(pl.program_id(1) == 0)
    def _():
        st_ref[...] = jnp.zeros_like(st_ref)

    lr = _dot(x_ref[...], wlr_ref[...]).astype(BF16)
    z = _dot(lr, wgu_ref[...]) + bg_ref[...]
    la_ref[...] = (jnp.minimum(z, 0.0) - jnp.log1p(jnp.exp(-jnp.abs(z)))) / GLA_GATE_NORM

    row = lax.broadcasted_iota(jnp.int32, (C, C), 0)
    col = lax.broadcasted_iota(jnp.int32, (C, C), 1)
    causal = row >= col
    tril = causal.astype(BF16)
    nw = nw_ref[...]

    nc = rows // C
    hs = [slice(h * LANES, (h + 1) * LANES) for h in range(GLA_HEADS)]
    rs = [slice(c * C, (c + 1) * C) for c in range(nc)]

    q_dec, k_inv, k_end, decay, vv = [], [], [], [], []
    for c in range(nc):
        la = la_ref[rs[c], :]
        hi = la.astype(BF16)
        r1 = la - hi.astype(F32)
        mid = r1.astype(BF16)
        lo = (r1 - mid.astype(F32)).astype(BF16)
        bcum = _dot(tril, hi) + _dot(tril, mid) + _dot(tril, lo)
        b_end = bcum[C - 1:C, :]
        q = q_ref[rs[c], :].astype(F32) * (GLA_DK ** -0.5)
        k = k_ref[rs[c], :].astype(F32)
        q_dec.append((q * jnp.exp(bcum)).astype(BF16))
        k_inv.append((k * jnp.exp(-bcum)).astype(BF16))
        k_end.append((k * jnp.exp(b_end - bcum)).astype(BF16))
        decay.append(jnp.exp(b_end))
        vv.append(v_ref[rs[c], :].astype(BF16))

    attn = [[jnp.where(causal, _dot_nt(q_dec[c][:, sl], k_inv[c][:, sl]), 0.0).astype(BF16)
             for sl in hs] for c in range(nc)]
    kv = [[_dot_tn(vv[c][:, sl], k_end[c][:, sl]) for sl in hs] for c in range(nc)]
    o_intra = [[_dot(attn[c][h], vv[c][:, hs[h]]) for h in range(GLA_HEADS)] for c in range(nc)]

    st = [st_ref[h] for h in range(GLA_HEADS)]
    for c in range(nc):
        g = g_ref[rs[c], :].astype(F32)
        for h, sl in enumerate(hs):
            o = o_intra[c][h] + _dot_nt(q_dec[c][:, sl], st[h].astype(BF16))
            st[h] = decay[c][:, sl] * st[h] + kv[c][h]
            o = o * lax.rsqrt(jnp.mean(o * o, axis=-1, keepdims=True) + RMS_EPS) * nw
            gh = g[:, sl]
            o = o * (gh * jax.nn.sigmoid(gh))
            o_ref[rs[c], sl] = o.astype(o_ref.dtype)
    for h in range(GLA_HEADS):
        st_ref[h] = st[h]


def _gla(xb, proj, w_lr, w_gu, b_gate, norm_w, layer, *, batch, rows):
    t = xb.shape[0]
    steps = t // batch // rows
    w = GLA_HEADS * GLA_DK

    def col(off):
        return pl.BlockSpec((rows, w), lambda b, s: (b * steps + s, off // w))

    def full(shape):
        return pl.BlockSpec((None,) + shape, lambda b, s: (layer,) + (0,) * len(shape))

    return pl.pallas_call(
        _gla_kernel,
        grid=(batch, steps),
        in_specs=[pl.BlockSpec((rows, D_MODEL), lambda b, s: (b * steps + s, 0)),
                  col(COL_GQ), col(COL_GK), col(COL_GV), col(COL_GG),
                  full((D_MODEL, LANES)), full((LANES, w)), full((1, w)), full((1, GLA_DV))],
        out_specs=pl.BlockSpec((rows, w), lambda b, s: (b * steps + s, 0)),
        out_shape=jax.ShapeDtypeStruct((t, w), BF16),
        scratch_shapes=[pltpu.VMEM((GLA_HEADS, GLA_DV, GLA_DK), F32),
                        pltpu.VMEM((rows, w), F32)],
        compiler_params=_params("arbitrary", "arbitrary"),
        name="gla",
    )(xb, proj, proj, proj, proj, w_lr, w_gu, b_gate, norm_w)


def _gmlp_kernel(z_ref, lng_ref, lnb_ref, ws_ref, bs_ref, o_ref):
    rows = z_ref.shape[0]
    C = GMLP_CHUNK
    z = z_ref[...].astype(F32)
    z = 0.5 * z * (1.0 + lax.erf(z * np.float32(np.sqrt(0.5))))
    u = z[:, :GMLP_WIDTH]
    v = _layer_norm(z[:, GMLP_WIDTH:], lng_ref[...], lnb_ref[...]).astype(BF16)
    row = lax.broadcasted_iota(jnp.int32, (C, C), 0)
    col = lax.broadcasted_iota(jnp.int32, (C, C), 1)
    tril = (row >= col).astype(F32)
    for g in range(GMLP_GROUPS):
        sl = slice(g * GMLP_GROUP_CH, (g + 1) * GMLP_GROUP_CH)
        w = (ws_ref[g] * tril).astype(BF16)
        for c in range(rows // C):
            rs = slice(c * C, (c + 1) * C)
            vs = _dot(w, v[rs, sl]) + bs_ref[:, sl]
            o_ref[rs, sl] = (u[rs, sl] * vs).astype(o_ref.dtype)


def _gmlp(proj, ln_g, ln_b, w_s, bs_full, layer, *, rows):
    t = proj.shape[0]

    def full(shape):
        return pl.BlockSpec((None,) + shape, lambda i: (layer,) + (0,) * len(shape))

    return pl.pallas_call(
        _gmlp_kernel,
        grid=(t // rows,),
        in_specs=[pl.BlockSpec((rows, 2 * GMLP_WIDTH), lambda i: (i, COL_GZ // (2 * GMLP_WIDTH))),
                  full((1, GMLP_WIDTH)), full((1, GMLP_WIDTH)),
                  full((GMLP_GROUPS, GMLP_CHUNK, GMLP_CHUNK)), full((GMLP_CHUNK, GMLP_WIDTH))],
        out_specs=pl.BlockSpec((rows, GMLP_WIDTH), lambda i: (i, 0)),
        out_shape=jax.ShapeDtypeStruct((t, GMLP_WIDTH), BF16),
        compiler_params=_params("parallel"),
        name="gmlp",
    )(proj, ln_g, ln_b, w_s, bs_full)


def _rope_table_kernel(pos_ref, inv_ref, c_ref, s1_ref, s2_ref):
    half = ROPE_DIMS // 2
    ang = pos_ref[...].astype(F32) * inv_ref[...]
    lane = lax.broadcasted_iota(jnp.int32, ang.shape, 1)
    cos = jnp.cos(ang)
    sin = jnp.sin(ang)
    c_ref[...] = jnp.where(lane < ROPE_DIMS, cos, 1.0)
    s1_ref[...] = jnp.where(lane < half, -sin, 0.0)
    s2_ref[...] = jnp.where((lane >= half) & (lane < ROPE_DIMS), sin, 0.0)


def _rope_tables(pos_col, inv_row, *, rows):
    t = pos_col.shape[0]
    spec = pl.BlockSpec((rows, LANES), lambda i: (i, 0))
    shp = jax.ShapeDtypeStruct((t, LANES), F32)
    return pl.pallas_call(
        _rope_table_kernel,
        grid=(t // rows,),
        in_specs=[pl.BlockSpec((rows, 1), lambda i: (i, 0)),
                  pl.BlockSpec((1, LANES), lambda i: (0, 0))],
        out_specs=[spec, spec, spec],
        out_shape=[shp, shp, shp],
        compiler_params=_params("parallel"),
        name="rope_tables",
    )(pos_col, inv_row)


def _moba_prep_kernel(q_ref, k_ref, v_ref, c_ref, s1_ref, s2_ref, qt_ref, ko_ref, vt_ref, km_ref):
    half = ROPE_DIMS // 2
    q_scale = np.float32(MOBA_HD ** -0.5 * np.log2(np.e))

    for n in range(q_ref.shape[0] // MOBA_BLOCK):
        rs = slice(n * MOBA_BLOCK, (n + 1) * MOBA_BLOCK)
        cos = c_ref[rs, :]
        s1 = s1_ref[rs, :]
        s2 = s2_ref[rs, :]

        def rope(t):
            return (t * cos + pltpu.roll(t, LANES - half, 1) * s1 + pltpu.roll(t, half, 1) * s2)

        for h in range(MOBA_HEADS):
            sl = slice(h * LANES, (h + 1) * LANES)
            qr = rope(q_ref[rs, sl].astype(F32)) * q_scale
            qt_ref[sl, rs] = qr.T.astype(qt_ref.dtype)
            kr = rope(k_ref[rs, sl].astype(F32))
            ko_ref[rs, sl] = kr.astype(ko_ref.dtype)
            km_ref[n, :, sl] = jnp.mean(kr, axis=0, keepdims=True)
            vt_ref[h * MOBA_VT_ROWS:h * MOBA_VT_ROWS + MOBA_HD, rs] = (
                v_ref[rs, sl].astype(F32).T.astype(vt_ref.dtype))
            vt_ref[h * MOBA_VT_ROWS + MOBA_HD:(h + 1) * MOBA_VT_ROWS, rs] = jnp.ones(
                (MOBA_VT_ROWS - MOBA_HD, MOBA_BLOCK), vt_ref.dtype)


def _moba_prep(proj3, cos3, s13, s23, *, rows):
    batch, seq, _ = proj3.shape
    w = MOBA_HEADS * MOBA_HD
    nb = seq // MOBA_BLOCK
    tab = pl.BlockSpec((None, rows, LANES), lambda b, i: (b, i, 0))

    def col(off):
        return pl.BlockSpec((None, rows, w), lambda b, i: (b, i, off // w))

    vw = MOBA_HEADS * MOBA_VT_ROWS
    return pl.pallas_call(
        _moba_prep_kernel,
        grid=(batch, seq // rows),
        in_specs=[col(COL_MQ), col(COL_MK), col(COL_MV), tab, tab, tab],
        out_specs=[pl.BlockSpec((None, w, rows), lambda b, i: (b, 0, i)),
                   pl.BlockSpec((None, rows, w), lambda b, i: (b, i, 0)),
                   pl.BlockSpec((None, vw, rows), lambda b, i: (b, 0, i)),
                   pl.BlockSpec((None, rows // MOBA_BLOCK, 1, w), lambda b, i: (b, i, 0, 0))],
        out_shape=[jax.ShapeDtypeStruct((batch, w, seq), BF16),
                   jax.ShapeDtypeStruct((batch, seq, w), BF16),
                   jax.ShapeDtypeStruct((batch, vw, seq), BF16),
                   jax.ShapeDtypeStruct((batch, nb, 1, w), F32)],
        compiler_params=_params("parallel", "parallel"),
        name="moba_prep",
    )(proj3, proj3, proj3, cos3, s13, s23)


def _moba_flash_kernel(qt_ref, k_ref, vt_ref, km_ref, o_ref, bias_ref, s_ref, acc_ref, *,
                       heads, tiles):
    BLK = MOBA_BLOCK
    first = pl.program_id(2) * tiles
    n_blk = km_ref.shape[1]
    streams = [(b, h, t) for b in range(qt_ref.shape[0]) for h in range(heads)
               for t in range(tiles)]
    blk = lax.broadcasted_iota(jnp.int32, (n_blk, BLK), 0)
    blk_f = blk.astype(F32)
    key = lax.broadcasted_iota(jnp.int32, (BLK, BLK), 0)
    qry = lax.broadcasted_iota(jnp.int32, (BLK, BLK), 1)
    hs = [slice(h * MOBA_HD, (h + 1) * MOBA_HD) for h in range(heads)]
    vs = [slice(h * MOBA_VT_ROWS, (h + 1) * MOBA_VT_ROWS) for h in range(heads)]
    ts = [slice(t * BLK, (t + 1) * BLK) for t in range(tiles)]

    def scores(b, h, t, start):
        return _dot(k_ref[b, pl.ds(start, BLK), hs[h]], qt_ref[b, hs[h], ts[t]])

    def past_block(n, b, h, m, s_max, j):
        bias = bias_ref[n, pl.ds(j, 1), :]
        m_new = jnp.maximum(m, s_max + bias)
        m_use = jnp.where(m_new == -jnp.inf, 0.0, m_new)
        alpha = jnp.exp2(m - m_use)
        p = jnp.exp2(s_ref[n] - (m_use - bias))
        c0 = pl.multiple_of(j * BLK, BLK)
        acc_ref[n] = alpha * acc_ref[n] + _dot(vt_ref[b, vs[h], pl.ds(c0, BLK)], p.astype(BF16))
        return m_new

    carry0 = []
    for n, (b, h, t) in enumerate(streams):
        own = first + t
        qt = qt_ref[b, hs[h], ts[t]]
        km = km_ref[b, :, hs[h]]
        km_hi = km.astype(BF16)
        km_lo = (km - km_hi.astype(F32)).astype(BF16)
        gate = _dot(km_hi, qt) + _dot(km_lo, qt)
        cand = blk < own
        sel = jnp.zeros(gate.shape, jnp.bool_)
        for r in range(min(MOBA_TOPK, n_blk)):
            g_eff = jnp.where(cand, gate, -jnp.inf)
            best = jnp.max(g_eff, axis=0, keepdims=True)
            idx = jnp.min(jnp.where(cand & (g_eff == best), blk_f, float(n_blk)),
                          axis=0, keepdims=True)
            idx = jnp.where(r < own, idx, float(n_blk))
            sel = sel | (blk_f == idx)
            cand = cand & (blk_f != idx)
        bias_ref[n] = jnp.where(sel, 0.0, -jnp.inf)
        s0 = scores(b, h, t, 0)
        s_ref[n] = s0
        acc_ref[n] = jnp.zeros((MOBA_VT_ROWS, BLK), F32)
        carry0 += [jnp.full((1, BLK), -jnp.inf, F32), jnp.max(s0, axis=0, keepdims=True)]

    def body(j, carry):
        n0 = pl.multiple_of((j + 1) * BLK, BLK)
        out = []
        for n, (b, h, t) in enumerate(streams):
            m, s_max = carry[2 * n:2 * n + 2]
            s_next = scores(b, h, t, n0)
            m_new = past_block(n, b, h, m, s_max, j)
            s_ref[n] = s_next
            out += [m_new, jnp.max(s_next, axis=0, keepdims=True)]
        return tuple(out)

    carry = list(lax.fori_loop(0, first, body, tuple(carry0)))
    for u in range(tiles):
        for n, (b, h, t) in enumerate(streams):
            m, s_max = carry[2 * n:2 * n + 2]
            if t == u:
                s = jnp.where(key <= qry, s_ref[n], -jnp.inf)
                m_new = jnp.maximum(m, jnp.max(s, axis=0, keepdims=True))
                alpha = jnp.exp2(m - m_new)
                p = jnp.exp2(s - m_new)
                r0 = pl.multiple_of((first + u) * BLK, BLK)
                acc = alpha * acc_ref[n] + _dot(vt_ref[b, vs[h], pl.ds(r0, BLK)], p.astype(BF16))
                o_ref[b, ts[t], hs[h]] = (
                    acc[:MOBA_HD] / acc[MOBA_HD:MOBA_HD + 1]).T.astype(o_ref.dtype)
            elif t > u:
                carry[2 * n] = past_block(n, b, h, m, s_max, first + u)
                s_next = scores(b, h, t, pl.multiple_of((first + u + 1) * BLK, BLK))
                s_ref[n] = s_next
                carry[2 * n + 1] = jnp.max(s_next, axis=0, keepdims=True)


def _moba_flash(q_t, k_rot, v_t, k_mean, *, rows, heads, tiles):
    batch, seq, w = k_rot.shape
    n_blk = seq // MOBA_BLOCK
    hw = heads * MOBA_HD
    tq = tiles * MOBA_BLOCK
    n_streams = rows * heads * tiles
    once = pl.Buffered(1)
    return pl.pallas_call(
        functools.partial(_moba_flash_kernel, heads=heads, tiles=tiles),
        grid=(batch // rows, MOBA_HEADS // heads, seq // tq),
        in_specs=[pl.BlockSpec((rows, hw, tq), lambda b, g, i: (b, g, i)),
                  pl.BlockSpec((rows, seq, hw), lambda b, g, i: (b, 0, g), pipeline_mode=once),
                  pl.BlockSpec((rows, heads * MOBA_VT_ROWS, seq), lambda b, g, i: (b, g, 0),
                               pipeline_mode=once),
                  pl.BlockSpec((rows, n_blk, hw), lambda b, g, i: (b, 0, g))],
        out_specs=pl.BlockSpec((rows, tq, hw), lambda b, g, i: (b, i, g)),
        out_shape=jax.ShapeDtypeStruct((batch, seq, w), BF16),
        scratch_shapes=[pltpu.VMEM((n_streams, n_blk, MOBA_BLOCK), F32),
                        pltpu.VMEM((n_streams, MOBA_BLOCK, MOBA_BLOCK), F32),
                        pltpu.VMEM((n_streams, MOBA_VT_ROWS, MOBA_BLOCK), F32)],
        compiler_params=_params("parallel", "parallel", "arbitrary"),
        name="moba_flash",
    )(q_t, k_rot, v_t, k_mean)


def _merge_kernel(a_ref, b_ref, c_ref, gl_ref, x_ref, wb_ref, wo_ref, g_ref, bt_ref,
                  xo_ref, xb_ref):
    gl = gl_ref[...].astype(F32)
    d = D_MODEL
    m = (jax.nn.sigmoid(gl[:, :d]) * _dot(a_ref[...], wb_ref[0])
         + jax.nn.sigmoid(gl[:, d:2 * d]) * _dot(b_ref[...], wb_ref[1])
         + jax.nn.sigmoid(gl[:, 2 * d:]) * _dot(c_ref[...], wb_ref[2]))
    mix = _dot(m.astype(BF16), wo_ref[...])
    y = _layer_norm(DEEPNORM_ALPHA * x_ref[...] + mix, g_ref[...], bt_ref[...])
    xo_ref[...] = y
    xb_ref[...] = y.astype(BF16)


def _merge(a, b, c, proj, x, w_branch, w_out, ln_g, ln_b, layer, *, tm):
    t = x.shape[0]
    br = pl.BlockSpec((tm, BRANCH_W), lambda i: (i, 0))
    xs = pl.BlockSpec((tm, D_MODEL), lambda i: (i, 0))

    def full(shape):
        return pl.BlockSpec((None,) + shape, lambda i: (layer,) + (0,) * len(shape))

    return pl.pallas_call(
        _merge_kernel,
        grid=(t // tm,),
        in_specs=[br, br, br,
                  pl.BlockSpec((tm, 3 * D_MODEL), lambda i: (i, COL_GL // (3 * D_MODEL))),
                  xs, full((3, BRANCH_W, D_MODEL)), full((D_MODEL, D_MODEL)),
                  full((1, D_MODEL)), full((1, D_MODEL))],
        out_specs=[xs, xs],
        out_shape=[jax.ShapeDtypeStruct((t, D_MODEL), F32),
                   jax.ShapeDtypeStruct((t, D_MODEL), BF16)],
        compiler_params=_params("parallel"),
        name="merge_ln1",
    )(a, b, c, proj, x, w_branch, w_out, ln_g, ln_b)


def _ffn_kernel(xb_ref, x_ref, w1_ref, w2_ref, g_ref, bt_ref, xo_ref, xbo_ref, acc_ref):
    f = pl.program_id(1)

    @pl.when(f == 0)
    def _():
        acc_ref[...] = jnp.zeros_like(acc_ref)

    h = jnp.square(jnp.maximum(_dot(xb_ref[...], w1_ref[...]), 0.0)).astype(BF16)
    acc_ref[...] += _dot(h, w2_ref[...])

    @pl.when(f == pl.num_programs(1) - 1)
    def _():
        y = _layer_norm(DEEPNORM_ALPHA * x_ref[...] + acc_ref[...], g_ref[...], bt_ref[...])
        xo_ref[...] = y
        xbo_ref[...] = y.astype(BF16)


def _ffn(xb, x, w1, w2, ln_g, ln_b, layer, *, tm, tf):
    t = x.shape[0]
    xs = pl.BlockSpec((tm, D_MODEL), lambda i, f: (i, 0))
    vec = pl.BlockSpec((None, 1, D_MODEL), lambda i, f: (layer, 0, 0))
    return pl.pallas_call(
        _ffn_kernel,
        grid=(t // tm, D_FF // tf),
        in_specs=[xs, xs,
                  pl.BlockSpec((None, D_MODEL, tf), lambda i, f: (layer, 0, f)),
                  pl.BlockSpec((None, tf, D_MODEL), lambda i, f: (layer, f, 0)),
                  vec, vec],
        out_specs=[xs, xs],
        out_shape=[jax.ShapeDtypeStruct((t, D_MODEL), F32),
                   jax.ShapeDtypeStruct((t, D_MODEL), BF16)],
        scratch_shapes=[pltpu.VMEM((tm, D_MODEL), F32)],
        compiler_params=_params("parallel", "arbitrary"),
        name="ffn_ln2",
    )(xb, x, w1, w2, ln_g, ln_b)


def _split_w_in(w_in):
    sizes = [512, 512, 512, 512, GLA_GATE_RANK, 1024, 512, 512, 512, 3 * D_MODEL]
    offs = np.concatenate([[0], np.cumsum(sizes)])
    gq, gk, gv, gg, glr, gz, mq, mk, mv, gl = [w_in[..., offs[i]:offs[i + 1]] for i in range(10)]
    w_main = jnp.concatenate([gl, gq, gk, gv, gg, gz, mq, mk, mv], axis=-1).astype(BF16)
    w_lr = jnp.pad(glr, ((0, 0), (0, 0), (0, LANES - GLA_GATE_RANK))).astype(BF16)
    return w_main, w_lr


def kernel(x, positions, w_in, w_gate_up, b_gate, gla_norm_w, gmlp_ln_g, gmlp_ln_b, gmlp_w_s,
           gmlp_b_s, w_branch, w_out, ln1_g, ln1_b, w_ff1, w_ff2, ln2_g, ln2_b):
    batch, seq, d = x.shape
    t = batch * seq
    depth = w_in.shape[0]
    assert d == D_MODEL and seq % 512 == 0

    half = ROPE_DIMS // 2
    inv = 1.0 / (ROPE_THETA ** (jnp.arange(half, dtype=F32) * (2.0 / ROPE_DIMS)))
    inv_row = jnp.concatenate([inv, inv, jnp.zeros((LANES - ROPE_DIMS,), F32)])[None, :]
    cos, s1, s2 = [a.reshape(batch, seq, LANES) for a in
                   _rope_tables(positions.reshape(t, 1), inv_row, rows=min(1024, seq))]

    w_main, w_lr = _split_w_in(w_in)
    w_gu = jnp.pad(w_gate_up, ((0, 0), (0, LANES - GLA_GATE_RANK), (0, 0))).astype(BF16)
    bs_full = jnp.repeat(jnp.swapaxes(gmlp_b_s, 1, 2), GMLP_GROUP_CH, axis=2)
    w_branch_b, w_out_b = w_branch.astype(BF16), w_out.astype(BF16)
    w_ff1_b, w_ff2_b = w_ff1.astype(BF16), w_ff2.astype(BF16)
    row = lambda v: v[:, None, :]

    xf = x.reshape(t, d)
    xb = xf.astype(BF16)
    for l in range(depth):
        proj = _matmul(xb, w_main, l, tm=2048, tn=1536, out_dtype=BF16)
        a = _gla(xb, proj, w_lr, w_gu, row(b_gate), row(gla_norm_w), l, batch=batch, rows=1024)
        b = _gmlp(proj, row(gmlp_ln_g), row(gmlp_ln_b), gmlp_w_s, bs_full, l, rows=512)
        q_t, k_rot, v_t, k_mean = _moba_prep(proj.reshape(batch, seq, -1), cos, s1, s2,
                                             rows=min(1024, seq))
        c = _moba_flash(q_t, k_rot, v_t, k_mean.reshape(batch, seq // MOBA_BLOCK, -1),
                        rows=2 if batch % 2 == 0 else 1, heads=MOBA_HEADS, tiles=2)
        xf, xb = _merge(a, b, c.reshape(t, -1), proj, xf, w_branch_b, w_out_b,
                        row(ln1_g), row(ln1_b), l, tm=512)
        xf, xb = _ffn(xb, xf, w_ff1_b, w_ff2_b, row(ln2_g), row(ln2_b), l, tm=1024, tf=1024)
    return xf.reshape(batch, seq, d)
```

```python
import functools

import numpy as np
import jax
import jax.numpy as jnp
from jax import lax
from jax.experimental import pallas as pl
from jax.experimental.pallas import tpu as pltpu

D_MODEL = 1024
DEPTH = 4
GLA_HEADS = 4
GLA_DK = 128
GLA_DV = 128
GLA_GATE_RANK = 16
GLA_GATE_NORM = 16.0
GLA_CHUNK = 64
GMLP_GROUPS = 4
GMLP_GROUP_CH = 128
GMLP_WIDTH = 512
GMLP_CHUNK = 128
MOBA_HEADS = 4
MOBA_HD = 128
MOBA_BLOCK = 256
MOBA_TOPK = 3
MOBA_VT_ROWS = MOBA_HD + 16
ROPE_THETA = 500000.0
ROPE_DIMS = MOBA_HD // 4
BRANCH_W = 512
D_FF = 4 * D_MODEL
DEEPNORM_ALPHA = (2 * DEPTH) ** 0.25
LN_EPS = 1e-5
RMS_EPS = 1e-6

LANES = 128
VMEM_LIMIT = 56 * 1024 * 1024

COL_GL = 0
COL_GQ = 3072
COL_GK = 3584
COL_GV = 4096
COL_GG = 4608
COL_GZ = 5120
COL_MQ = 6144
COL_MK = 6656
COL_MV = 7168
PROJ_COLS = 7680

BF16 = jnp.bfloat16
F32 = jnp.float32


def _dot(a, b):
    return jnp.dot(a, b, preferred_element_type=F32)


def _dot_nt(a, b):
    return lax.dot_general(a, b, (((1,), (1,)), ((), ())), preferred_element_type=F32)


def _dot_tn(a, b):
    return lax.dot_general(a, b, (((0,), (0,)), ((), ())), preferred_element_type=F32)


def _params(*sem):
    return pltpu.CompilerParams(dimension_semantics=sem, vmem_limit_bytes=VMEM_LIMIT)


def _layer_norm(y, g, b):
    mu = jnp.mean(y, axis=-1, keepdims=True)
    yc = y - mu
    var = jnp.mean(yc * yc, axis=-1, keepdims=True)
    return yc * lax.rsqrt(var + LN_EPS) * g + b


def _matmul_kernel(x_ref, w_ref, o_ref):
    o_ref[...] = _dot(x_ref[...], w_ref[...]).astype(o_ref.dtype)


def _matmul(x, w, layer, *, tm, tn, out_dtype):
    m, k = x.shape
    n = w.shape[2]
    return pl.pallas_call(
        _matmul_kernel,
        grid=(m // tm, n // tn),
        in_specs=[pl.BlockSpec((tm, k), lambda i, j: (i, 0)),
                  pl.BlockSpec((None, k, tn), lambda i, j: (layer, 0, j))],
        out_specs=pl.BlockSpec((tm, tn), lambda i, j: (i, j)),
        out_shape=jax.ShapeDtypeStruct((m, n), out_dtype),
        compiler_params=_params("parallel", "parallel"),
        name="in_proj",
    )(x, w)


def _gla_kernel(x_ref, q_ref, k_ref, v_ref, g_ref, wlr_ref, wgu_ref, bg_ref, nw_ref,
                o_ref, st_ref, la_ref):
    rows = x_ref.shape[0]
    C = GLA_CHUNK

    @pl.when(pl.program_id(1) == 0)
    def _():
        st_ref[...] = jnp.zeros_like(st_ref)

    lr = _dot(x_ref[...], wlr_ref[...]).astype(BF16)
    z = _dot(lr, wgu_ref[...]) + bg_ref[...]
    la_ref[...] = (jnp.minimum(z, 0.0) - jnp.log1p(jnp.exp(-jnp.abs(z)))) / GLA_GATE_NORM

    row = lax.broadcasted_iota(jnp.int32, (C, C), 0)
    col = lax.broadcasted_iota(jnp.int32, (C, C), 1)
    causal = row >= col
    tril = causal.astype(BF16)
    nw = nw_ref[...]

    nc = rows // C
    hs = [slice(h * LANES, (h + 1) * LANES) for h in range(GLA_HEADS)]
    rs = [slice(c * C, (c + 1) * C) for c in range(nc)]

    q_dec, k_inv, k_end, decay, vv = [], [], [], [], []
    for c in range(nc):
        la = la_ref[rs[c], :]
        hi = la.astype(BF16)
        r1 = la - hi.astype(F32)
        mid = r1.astype(BF16)
        lo = (r1 - mid.astype(F32)).astype(BF16)
        bcum = _dot(tril, hi) + _dot(tril, mid) + _dot(tril, lo)
        b_end = bcum[C - 1:C, :]
        q = q_ref[rs[c], :].astype(F32) * (GLA_DK ** -0.5)
        k = k_ref[rs[c], :].astype(F32)
        q_dec.append((q * jnp.exp(bcum)).astype(BF16))
        k_inv.append((k * jnp.exp(-bcum)).astype(BF16))
        k_end.append((k * jnp.exp(b_end - bcum)).astype(BF16))
        decay.append(jnp.exp(b_end))
        vv.append(v_ref[rs[c], :].astype(BF16))

    attn = [[jnp.where(causal, _dot_nt(q_dec[c][:, sl], k_inv[c][:, sl]), 0.0).astype(BF16)
             for sl in hs] for c in range(nc)]
    kv = [[_dot_tn(vv[c][:, sl], k_end[c][:, sl]) for sl in hs] for c in range(nc)]
    o_intra = [[_dot(attn[c][h], vv[c][:, hs[h]]) for h in range(GLA_HEADS)] for c in range(nc)]

    st = [st_ref[h] for h in range(GLA_HEADS)]
    for c in range(nc):
        g = g_ref[rs[c], :].astype(F32)
        for h, sl in enumerate(hs):
            o = o_intra[c][h] + _dot_nt(q_dec[c][:, sl], st[h].astype(BF16))
            st[h] = decay[c][:, sl] * st[h] + kv[c][h]
            o = o * lax.rsqrt(jnp.mean(o * o, axis=-1, keepdims=True) + RMS_EPS) * nw
            gh = g[:, sl]
            o = o * (gh * jax.nn.sigmoid(gh))
            o_ref[rs[c], sl] = o.astype(o_ref.dtype)
    for h in range(GLA_HEADS):
        st_ref[h] = st[h]


def _gla(xb, proj, w_lr, w_gu, b_gate, norm_w, layer, *, batch, rows):
    t = xb.shape[0]
    steps = t // batch // rows
    w = GLA_HEADS * GLA_DK

    def col(off):
        return pl.BlockSpec((rows, w), lambda b, s: (b * steps + s, off // w))

    def full(shape):
        return pl.BlockSpec((None,) + shape, lambda b, s: (layer,) + (0,) * len(shape))

    return pl.pallas_call(
        _gla_kernel,
        grid=(batch, steps),
        in_specs=[pl.BlockSpec((rows, D_MODEL), lambda b, s: (b * steps + s, 0)),
                  col(COL_GQ), col(COL_GK), col(COL_GV), col(COL_GG),
                  full((D_MODEL, LANES)), full((LANES, w)), full((1, w)), full((1, GLA_DV))],
        out_specs=pl.BlockSpec((rows, w), lambda b, s: (b * steps + s, 0)),
        out_shape=jax.ShapeDtypeStruct((t, w), BF16),
        scratch_shapes=[pltpu.VMEM((GLA_HEADS, GLA_DV, GLA_DK), F32),
                        pltpu.VMEM((rows, w), F32)],
        compiler_params=_params("arbitrary", "arbitrary"),
        name="gla",
    )(xb, proj, proj, proj, proj, w_lr, w_gu, b_gate, norm_w)


def _gmlp_kernel(z_ref, lng_ref, lnb_ref, ws_ref, bs_ref, o_ref):
    rows = z_ref.shape[0]
    C = GMLP_CHUNK
    z = z_ref[...].astype(F32)
    z = 0.5 * z * (1.0 + lax.erf(z * np.float32(np.sqrt(0.5))))
    u = z[:, :GMLP_WIDTH]
    v = _layer_norm(z[:, GMLP_WIDTH:], lng_ref[...], lnb_ref[...]).astype(BF16)
    row = lax.broadcasted_iota(jnp.int32, (C, C), 0)
    col = lax.broadcasted_iota(jnp.int32, (C, C), 1)
    tril = (row >= col).astype(F32)
    for g in range(GMLP_GROUPS):
        sl = slice(g * GMLP_GROUP_CH, (g + 1) * GMLP_GROUP_CH)
        w = (ws_ref[g] * tril).astype(BF16)
        for c in range(rows // C):
            rs = slice(c * C, (c + 1) * C)
            vs = _dot(w, v[rs, sl]) + bs_ref[:, sl]
            o_ref[rs, sl] = (u[rs, sl] * vs).astype(o_ref.dtype)


def _gmlp(proj, ln_g, ln_b, w_s, bs_full, layer, *, rows):
    t = proj.shape[0]

    def full(shape):
        return pl.BlockSpec((None,) + shape, lambda i: (layer,) + (0,) * len(shape))

    return pl.pallas_call(
        _gmlp_kernel,
        grid=(t // rows,),
        in_specs=[pl.BlockSpec((rows, 2 * GMLP_WIDTH), lambda i: (i, COL_GZ // (2 * GMLP_WIDTH))),
                  full((1, GMLP_WIDTH)), full((1, GMLP_WIDTH)),
                  full((GMLP_GROUPS, GMLP_CHUNK, GMLP_CHUNK)), full((GMLP_CHUNK, GMLP_WIDTH))],
        out_specs=pl.BlockSpec((rows, GMLP_WIDTH), lambda i: (i, 0)),
        out_shape=jax.ShapeDtypeStruct((t, GMLP_WIDTH), BF16),
        compiler_params=_params("parallel"),
        name="gmlp",
    )(proj, ln_g, ln_b, w_s, bs_full)


def _rope_table_kernel(pos_ref, inv_ref, c_ref, s1_ref, s2_ref):
    half = ROPE_DIMS // 2
    ang = pos_ref[...].astype(F32) * inv_ref[...]
    lane = lax.broadcasted_iota(jnp.int32, ang.shape, 1)
    cos = jnp.cos(ang)
    sin = jnp.sin(ang)
    c_ref[...] = jnp.where(lane < ROPE_DIMS, cos, 1.0)
    s1_ref[...] = jnp.where(lane < half, -sin, 0.0)
    s2_ref[...] = jnp.where((lane >= half) & (lane < ROPE_DIMS), sin, 0.0)


def _rope_tables(pos_col, inv_row, *, rows):
    t = pos_col.shape[0]
    spec = pl.BlockSpec((rows, LANES), lambda i: (i, 0))
    shp = jax.ShapeDtypeStruct((t, LANES), F32)
    return pl.pallas_call(
        _rope_table_kernel,
        grid=(t // rows,),
        in_specs=[pl.BlockSpec((rows, 1), lambda i: (i, 0)),
                  pl.BlockSpec((1, LANES), lambda i: (0, 0))],
        out_specs=[spec, spec, spec],
        out_shape=[shp, shp, shp],
        compiler_params=_params("parallel"),
        name="rope_tables",
    )(pos_col, inv_row)


def _moba_prep_kernel(q_ref, k_ref, v_ref, c_ref, s1_ref, s2_ref, qt_ref, ko_ref, vt_ref, km_ref):
    half = ROPE_DIMS // 2
    q_scale = np.float32(MOBA_HD ** -0.5 * np.log2(np.e))

    for n in range(q_ref.shape[0] // MOBA_BLOCK):
        rs = slice(n * MOBA_BLOCK, (n + 1) * MOBA_BLOCK)
        cos = c_ref[rs, :]
        s1 = s1_ref[rs, :]
        s2 = s2_ref[rs, :]

        def rope(t):
            return (t * cos + pltpu.roll(t, LANES - half, 1) * s1 + pltpu.roll(t, half, 1) * s2)

        for h in range(MOBA_HEADS):
            sl = slice(h * LANES, (h + 1) * LANES)
            qr = rope(q_ref[rs, sl].astype(F32)) * q_scale
            qt_ref[sl, rs] = qr.T.astype(qt_ref.dtype)
            kr = rope(k_ref[rs, sl].astype(F32))
            ko_ref[rs, sl] = kr.astype(ko_ref.dtype)
            km_ref[n, :, sl] = jnp.mean(kr, axis=0, keepdims=True)
            vt_ref[h * MOBA_VT_ROWS:h * MOBA_VT_ROWS + MOBA_HD, rs] = (
                v_ref[rs, sl].astype(F32).T.astype(vt_ref.dtype))
            vt_ref[h * MOBA_VT_ROWS + MOBA_HD:(h + 1) * MOBA_VT_ROWS, rs] = jnp.ones(
                (MOBA_VT_ROWS - MOBA_HD, MOBA_BLOCK), vt_ref.dtype)


def _moba_prep(proj3, cos3, s13, s23, *, rows):
    batch, seq, _ = proj3.shape
    w = MOBA_HEADS * MOBA_HD
    nb = seq // MOBA_BLOCK
    tab = pl.BlockSpec((None, rows, LANES), lambda b, i: (b, i, 0))

    def col(off):
        return pl.BlockSpec((None, rows, w), lambda b, i: (b, i, off // w))

    vw = MOBA_HEADS * MOBA_VT_ROWS
    return pl.pallas_call(
        _moba_prep_kernel,
        grid=(batch, seq // rows),
        in_specs=[col(COL_MQ), col(COL_MK), col(COL_MV), tab, tab, tab],
        out_specs=[pl.BlockSpec((None, w, rows), lambda b, i: (b, 0, i)),
                   pl.BlockSpec((None, rows, w), lambda b, i: (b, i, 0)),
                   pl.BlockSpec((None, vw, rows), lambda b, i: (b, 0, i)),
                   pl.BlockSpec((None, rows // MOBA_BLOCK, 1, w), lambda b, i: (b, i, 0, 0))],
        out_shape=[jax.ShapeDtypeStruct((batch, w, seq), BF16),
                   jax.ShapeDtypeStruct((batch, seq, w), BF16),
                   jax.ShapeDtypeStruct((batch, vw, seq), BF16),
                   jax.ShapeDtypeStruct((batch, nb, 1, w), F32)],
        compiler_params=_params("parallel", "parallel"),
        name="moba_prep",
    )(proj3, proj3, proj3, cos3, s13, s23)


def _moba_flash_kernel(qt_ref, k_ref, vt_ref, km_ref, o_ref, bias_ref, s_ref, acc_ref, *,
                       heads, tiles):
    BLK = MOBA_BLOCK
    first = pl.program_id(2) * tiles
    n_blk = km_ref.shape[1]
    streams = [(b, h, t) for b in range(qt_ref.shape[0]) for h in range(heads)
               for t in range(tiles)]
    blk = lax.broadcasted_iota(jnp.int32, (n_blk, BLK), 0)
    blk_f = blk.astype(F32)
    key = lax.broadcasted_iota(jnp.int32, (BLK, BLK), 0)
    qry = lax.broadcasted_iota(jnp.int32, (BLK, BLK), 1)
    hs = [slice(h * MOBA_HD, (h + 1) * MOBA_HD) for h in range(heads)]
    vs = [slice(h * MOBA_VT_ROWS, (h + 1) * MOBA_VT_ROWS) for h in range(heads)]
    ts = [slice(t * BLK, (t + 1) * BLK) for t in range(tiles)]

    def scores(b, h, t, start):
        return _dot(k_ref[b, pl.ds(start, BLK), hs[h]], qt_ref[b, hs[h], ts[t]])

    def past_block(n, b, h, m, s_max, j):
        bias = bias_ref[n, pl.ds(j, 1), :]
        m_new = jnp.maximum(m, s_max + bias)
        m_use = jnp.where(m_new == -jnp.inf, 0.0, m_new)
        alpha = jnp.exp2(m - m_use)
        p = jnp.exp2(s_ref[n] - (m_use - bias))
        c0 = pl.multiple_of(j * BLK, BLK)
        acc_ref[n] = alpha * acc_ref[n] + _dot(vt_ref[b, vs[h], pl.ds(c0, BLK)], p.astype(BF16))
        return m_new

    carry0 = []
    for n, (b, h, t) in enumerate(streams):
        own = first + t
        qt = qt_ref[b, hs[h], ts[t]]
        km = km_ref[b, :, hs[h]]
        km_hi = km.astype(BF16)
        km_lo = (km - km_hi.astype(F32)).astype(BF16)
        gate = _dot(km_hi, qt) + _dot(km_lo, qt)
        cand = blk < own
        sel = jnp.zeros(gate.shape, jnp.bool_)
        for r in range(min(MOBA_TOPK, n_blk)):
            g_eff = jnp.where(cand, gate, -jnp.inf)
            best = jnp.max(g_eff, axis=0, keepdims=True)
            idx = jnp.min(jnp.where(cand & (g_eff == best), blk_f, float(n_blk)),
                          axis=0, keepdims=True)
            idx = jnp.where(r < own, idx, float(n_blk))
            sel = sel | (blk_f == idx)
            cand = cand & (blk_f != idx)
        bias_ref[n] = jnp.where(sel, 0.0, -jnp.inf)
        s0 = scores(b, h, t, 0)
        s_ref[n] = s0
        acc_ref[n] = jnp.zeros((MOBA_VT_ROWS, BLK), F32)
        carry0 += [jnp.full((1, BLK), -jnp.inf, F32), jnp.max(s0, axis=0, keepdims=True)]

    def body(j, carry):
        n0 = pl.multiple_of((j + 1) * BLK, BLK)
        out = []
        for n, (b, h, t) in enumerate(streams):
            m, s_max = carry[2 * n:2 * n + 2]
            s_next = scores(b, h, t, n0)
            m_new = past_block(n, b, h, m, s_max, j)
            s_ref[n] = s_next
            out += [m_new, jnp.max(s_next, axis=0, keepdims=True)]
        return tuple(out)

    carry = list(lax.fori_loop(0, first, body, tuple(carry0)))
    for u in range(tiles):
        for n, (b, h, t) in enumerate(streams):
            m, s_max = carry[2 * n:2 * n + 2]
            if t == u:
                s = jnp.where(key <= qry, s_ref[n], -jnp.inf)
                m_new = jnp.maximum(m, jnp.max(s, axis=0, keepdims=True))
                alpha = jnp.exp2(m - m_new)
                p = jnp.exp2(s - m_new)
                r0 = pl.multiple_of((first + u) * BLK, BLK)
                acc = alpha * acc_ref[n] + _dot(vt_ref[b, vs[h], pl.ds(r0, BLK)], p.astype(BF16))
                o_ref[b, ts[t], hs[h]] = (
                    acc[:MOBA_HD] / acc[MOBA_HD:MOBA_HD + 1]).T.astype(o_ref.dtype)
            elif t > u:
                carry[2 * n] = past_block(n, b, h, m, s_max, first + u)
                s_next = scores(b, h, t, pl.multiple_of((first + u + 1) * BLK, BLK))
                s_ref[n] = s_next
                carry[2 * n + 1] = jnp.max(s_next, axis=0, keepdims=True)


def _moba_flash(q_t, k_rot, v_t, k_mean, *, rows, heads, tiles):
    batch, seq, w = k_rot.shape
    n_blk = seq // MOBA_BLOCK
    hw = heads * MOBA_HD
    tq = tiles * MOBA_BLOCK
    n_streams = rows * heads * tiles
    once = pl.Buffered(1)
    return pl.pallas_call(
        functools.partial(_moba_flash_kernel, heads=heads, tiles=tiles),
        grid=(batch // rows, MOBA_HEADS // heads, seq // tq),
        in_specs=[pl.BlockSpec((rows, hw, tq), lambda b, g, i: (b, g, i)),
                  pl.BlockSpec((rows, seq, hw), lambda b, g, i: (b, 0, g), pipeline_mode=once),
                  pl.BlockSpec((rows, heads * MOBA_VT_ROWS, seq), lambda b, g, i: (b, g, 0),
                               pipeline_mode=once),
                  pl.BlockSpec((rows, n_blk, hw), lambda b, g, i: (b, 0, g))],
        out_specs=pl.BlockSpec((rows, tq, hw), lambda b, g, i: (b, i, g)),
        out_shape=jax.ShapeDtypeStruct((batch, seq, w), BF16),
        scratch_shapes=[pltpu.VMEM((n_streams, n_blk, MOBA_BLOCK), F32),
                        pltpu.VMEM((n_streams, MOBA_BLOCK, MOBA_BLOCK), F32),
                        pltpu.VMEM((n_streams, MOBA_VT_ROWS, MOBA_BLOCK), F32)],
        compiler_params=_params("parallel", "parallel", "arbitrary"),
        name="moba_flash",
    )(q_t, k_rot, v_t, k_mean)


def _merge_kernel(a_ref, b_ref, c_ref, gl_ref, x_ref, wb_ref, wo_ref, g_ref, bt_ref,
                  xo_ref, xb_ref):
    gl = gl_ref[...].astype(F32)
    d = D_MODEL
    m = (jax.nn.sigmoid(gl[:, :d]) * _dot(a_ref[...], wb_ref[0])
         + jax.nn.sigmoid(gl[:, d:2 * d]) * _dot(b_ref[...], wb_ref[1])
         + jax.nn.sigmoid(gl[:, 2 * d:]) * _dot(c_ref[...], wb_ref[2]))
    mix = _dot(m.astype(BF16), wo_ref[...])
    y = _layer_norm(DEEPNORM_ALPHA * x_ref[...] + mix, g_ref[...], bt_ref[...])
    xo_ref[...] = y
    xb_ref[...] = y.astype(BF16)


def _merge(a, b, c, proj, x, w_branch, w_out, ln_g, ln_b, layer, *, tm):
    t = x.shape[0]
    br = pl.BlockSpec((tm, BRANCH_W), lambda i: (i, 0))
    xs = pl.BlockSpec((tm, D_MODEL), lambda i: (i, 0))

    def full(shape):
        return pl.BlockSpec((None,) + shape, lambda i: (layer,) + (0,) * len(shape))

    return pl.pallas_call(
        _merge_kernel,
        grid=(t // tm,),
        in_specs=[br, br, br,
                  pl.BlockSpec((tm, 3 * D_MODEL), lambda i: (i, COL_GL // (3 * D_MODEL))),
                  xs, full((3, BRANCH_W, D_MODEL)), full((D_MODEL, D_MODEL)),
                  full((1, D_MODEL)), full((1, D_MODEL))],
        out_specs=[xs, xs],
        out_shape=[jax.ShapeDtypeStruct((t, D_MODEL), F32),
                   jax.ShapeDtypeStruct((t, D_MODEL), BF16)],
        compiler_params=_params("parallel"),
        name="merge_ln1",
    )(a, b, c, proj, x, w_branch, w_out, ln_g, ln_b)


def _ffn_kernel(xb_ref, x_ref, w1_ref, w2_ref, g_ref, bt_ref, xo_ref, xbo_ref):
    h = jnp.square(jnp.maximum(_dot(xb_ref[...], w1_ref[...]), 0.0)).astype(BF16)
    ff = _dot(h, w2_ref[...])
    y = _layer_norm(DEEPNORM_ALPHA * x_ref[...] + ff, g_ref[...], bt_ref[...])
    xo_ref[...] = y
    xbo_ref[...] = y.astype(BF16)


def _ffn(xb, x, w1, w2, ln_g, ln_b, layer, *, tm):
    t = x.shape[0]
    xs = pl.BlockSpec((tm, D_MODEL), lambda i: (i, 0))
    vec = pl.BlockSpec((None, 1, D_MODEL), lambda i: (layer, 0, 0))
    once = pl.Buffered(1)
    return pl.pallas_call(
        _ffn_kernel,
        grid=(t // tm,),
        in_specs=[xs, xs,
                  pl.BlockSpec((None, D_MODEL, D_FF), lambda i: (layer, 0, 0), pipeline_mode=once),
                  pl.BlockSpec((None, D_FF, D_MODEL), lambda i: (layer, 0, 0), pipeline_mode=once),
                  vec, vec],
        out_specs=[xs, xs],
        out_shape=[jax.ShapeDtypeStruct((t, D_MODEL), F32),
                   jax.ShapeDtypeStruct((t, D_MODEL), BF16)],
        compiler_params=_params("parallel"),
        name="ffn_ln2",
    )(xb, x, w1, w2, ln_g, ln_b)


def _split_w_in(w_in):
    sizes = [512, 512, 512, 512, GLA_GATE_RANK, 1024, 512, 512, 512, 3 * D_MODEL]
    offs = np.concatenate([[0], np.cumsum(sizes)])
    gq, gk, gv, gg, glr, gz, mq, mk, mv, gl = [w_in[..., offs[i]:offs[i + 1]] for i in range(10)]
    w_main = jnp.concatenate([gl, gq, gk, gv, gg, gz, mq, mk, mv], axis=-1).astype(BF16)
    w_lr = jnp.pad(glr, ((0, 0), (0, 0), (0, LANES - GLA_GATE_RANK))).astype(BF16)
    return w_main, w_lr


def kernel(x, positions, w_in, w_gate_up, b_gate, gla_norm_w, gmlp_ln_g, gmlp_ln_b, gmlp_w_s,
           gmlp_b_s, w_branch, w_out, ln1_g, ln1_b, w_ff1, w_ff2, ln2_g, ln2_b):
    batch, seq, d = x.shape
    t = batch * seq
    depth = w_in.shape[0]
    assert d == D_MODEL and seq % 512 == 0

    half = ROPE_DIMS // 2
    inv = 1.0 / (ROPE_THETA ** (jnp.arange(half, dtype=F32) * (2.0 / ROPE_DIMS)))
    inv_row = jnp.concatenate([inv, inv, jnp.zeros((LANES - ROPE_DIMS,), F32)])[None, :]
    cos, s1, s2 = [a.reshape(batch, seq, LANES) for a in
                   _rope_tables(positions.reshape(t, 1), inv_row, rows=min(1024, seq))]

    w_main, w_lr = _split_w_in(w_in)
    w_gu = jnp.pad(w_gate_up, ((0, 0), (0, LANES - GLA_GATE_RANK), (0, 0))).astype(BF16)
    bs_full = jnp.repeat(jnp.swapaxes(gmlp_b_s, 1, 2), GMLP_GROUP_CH, axis=2)
    w_branch_b, w_out_b = w_branch.astype(BF16), w_out.astype(BF16)
    w_ff1_b, w_ff2_b = w_ff1.astype(BF16), w_ff2.astype(BF16)
    row = lambda v: v[:, None, :]

    xf = x.reshape(t, d)
    xb = xf.astype(BF16)
    for l in range(depth):
        proj = _matmul(xb, w_main, l, tm=2048, tn=1536, out_dtype=BF16)
        a = _gla(xb, proj, w_lr, w_gu, row(b_gate), row(gla_norm_w), l, batch=batch, rows=1024)
        b = _gmlp(proj, row(gmlp_ln_g), row(gmlp_ln_b), gmlp_w_s, bs_full, l, rows=512)
        q_t, k_rot, v_t, k_mean = _moba_prep(proj.reshape(batch, seq, -1), cos, s1, s2,
                                             rows=min(1024, seq))
        c = _moba_flash(q_t, k_rot, v_t, k_mean.reshape(batch, seq // MOBA_BLOCK, -1),
                        rows=2 if batch % 2 == 0 else 1, heads=MOBA_HEADS, tiles=2)
        xf, xb = _merge(a, b, c.reshape(t, -1), proj, xf, w_branch_b, w_out_b,
                        row(ln1_g), row(ln1_b), l, tm=512)
        xf, xb = _ffn(xb, xf, w_ff1_b, w_ff2_b, row(ln2_g), row(ln2_b), l, tm=512)
    return xf.reshape(batch, seq, d)
```

```python
import functools

import numpy as np
import jax
import jax.numpy as jnp
from jax import lax
from jax.experimental import pallas as pl
from jax.experimental.pallas import tpu as pltpu

D_MODEL = 1024
DEPTH = 4
GLA_HEADS = 4
GLA_DK = 128
GLA_DV = 128
GLA_GATE_RANK = 16
GLA_GATE_NORM = 16.0
GLA_CHUNK = 64
GMLP_GROUPS = 4
GMLP_GROUP_CH = 128
GMLP_WIDTH = 512
GMLP_CHUNK = 128
MOBA_HEADS = 4
MOBA_HD = 128
MOBA_BLOCK = 256
MOBA_TOPK = 3
MOBA_VT_ROWS = MOBA_HD + 16
ROPE_THETA = 500000.0
ROPE_DIMS = MOBA_HD // 4
BRANCH_W = 512
D_FF = 4 * D_MODEL
DEEPNORM_ALPHA = (2 * DEPTH) ** 0.25
LN_EPS = 1e-5
RMS_EPS = 1e-6

LANES = 128
VMEM_LIMIT = 56 * 1024 * 1024

COL_GL = 0
COL_GQ = 3072
COL_GK = 3584
COL_GV = 4096
COL_GG = 4608
COL_GZ = 5120
COL_MQ = 6144
COL_MK = 6656
COL_MV = 7168
PROJ_COLS = 7680

BF16 = jnp.bfloat16
F32 = jnp.float32


def _dot(a, b):
    return jnp.dot(a, b, preferred_element_type=F32)


def _dot_nt(a, b):
    return lax.dot_general(a, b, (((1,), (1,)), ((), ())), preferred_element_type=F32)


def _dot_tn(a, b):
    return lax.dot_general(a, b, (((0,), (0,)), ((), ())), preferred_element_type=F32)


def _params(*sem):
    return pltpu.CompilerParams(dimension_semantics=sem, vmem_limit_bytes=VMEM_LIMIT)


def _layer_norm(y, g, b):
    mu = jnp.mean(y, axis=-1, keepdims=True)
    yc = y - mu
    var = jnp.mean(yc * yc, axis=-1, keepdims=True)
    return yc * lax.rsqrt(var + LN_EPS) * g + b


def _matmul_kernel(x_ref, w_ref, o_ref):
    o_ref[...] = _dot(x_ref[...], w_ref[...]).astype(o_ref.dtype)


def _matmul(x, w, layer, *, tm, tn, out_dtype):
    m, k = x.shape
    n = w.shape[2]
    return pl.pallas_call(
        _matmul_kernel,
        grid=(m // tm, n // tn),
        in_specs=[pl.BlockSpec((tm, k), lambda i, j: (i, 0)),
                  pl.BlockSpec((None, k, tn), lambda i, j: (layer, 0, j))],
        out_specs=pl.BlockSpec((tm, tn), lambda i, j: (i, j)),
        out_shape=jax.ShapeDtypeStruct((m, n), out_dtype),
        compiler_params=_params("parallel", "parallel"),
        name="in_proj",
    )(x, w)


def _gla_kernel(x_ref, q_ref, k_ref, v_ref, g_ref, wlr_ref, wgu_ref, bg_ref, nw_ref,
                o_ref, st_ref, la_ref):
    rows = x_ref.shape[0]
    C = GLA_CHUNK

    @pl.when(pl.program_id(1) == 0)
    def _():
        st_ref[...] = jnp.zeros_like(st_ref)

    lr = _dot(x_ref[...], wlr_ref[...]).astype(BF16)
    z = _dot(lr, wgu_ref[...]) + bg_ref[...]
    la_ref[...] = (jnp.minimum(z, 0.0) - jnp.log1p(jnp.exp(-jnp.abs(z)))) / GLA_GATE_NORM

    row = lax.broadcasted_iota(jnp.int32, (C, C), 0)
    col = lax.broadcasted_iota(jnp.int32, (C, C), 1)
    causal = row >= col
    tril = causal.astype(BF16)
    nw = nw_ref[...]

    nc = rows // C
    hs = [slice(h * LANES, (h + 1) * LANES) for h in range(GLA_HEADS)]
    rs = [slice(c * C, (c + 1) * C) for c in range(nc)]

    q_dec, k_inv, k_end, decay, vv = [], [], [], [], []
    for c in range(nc):
        la = la_ref[rs[c], :]
        hi = la.astype(BF16)
        r1 = la - hi.astype(F32)
        mid = r1.astype(BF16)
        lo = (r1 - mid.astype(F32)).astype(BF16)
        bcum = _dot(tril, hi) + _dot(tril, mid) + _dot(tril, lo)
        b_end = bcum[C - 1:C, :]
        q = q_ref[rs[c], :].astype(F32) * (GLA_DK ** -0.5)
        k = k_ref[rs[c], :].astype(F32)
        q_dec.append((q * jnp.exp(bcum)).astype(BF16))
        k_inv.append((k * jnp.exp(-bcum)).astype(BF16))
        k_end.append((k * jnp.exp(b_end - bcum)).astype(BF16))
        decay.append(jnp.exp(b_end))
        vv.append(v_ref[rs[c], :].astype(BF16))

    attn = [[jnp.where(causal, _dot_nt(q_dec[c][:, sl], k_inv[c][:, sl]), 0.0).astype(BF16)
             for sl in hs] for c in range(nc)]
    kv = [[_dot_tn(vv[c][:, sl], k_end[c][:, sl]) for sl in hs] for c in range(nc)]
    o_intra = [[_dot(attn[c][h], vv[c][:, hs[h]]) for h in range(GLA_HEADS)] for c in range(nc)]

    st = [st_ref[h] for h in range(GLA_HEADS)]
    for c in range(nc):
        g = g_ref[rs[c], :].astype(F32)
        for h, sl in enumerate(hs):
            o = o_intra[c][h] + _dot_nt(q_dec[c][:, sl], st[h].astype(BF16))
            st[h] = decay[c][:, sl] * st[h] + kv[c][h]
            o = o * lax.rsqrt(jnp.mean(o * o, axis=-1, keepdims=True) + RMS_EPS) * nw
            gh = g[:, sl]
            o = o * (gh * jax.nn.sigmoid(gh))
            o_ref[rs[c], sl] = o.astype(o_ref.dtype)
    for h in range(GLA_HEADS):
        st_ref[h] = st[h]


def _gla(xb, proj, w_lr, w_gu, b_gate, norm_w, layer, *, batch, rows):
    t = xb.shape[0]
    steps = t // batch // rows
    w = GLA_HEADS * GLA_DK

    def col(off):
        return pl.BlockSpec((rows, w), lambda b, s: (b * steps + s, off // w))

    def full(shape):
        return pl.BlockSpec((None,) + shape, lambda b, s: (layer,) + (0,) * len(shape))

    return pl.pallas_call(
        _gla_kernel,
        grid=(batch, steps),
        in_specs=[pl.BlockSpec((rows, D_MODEL), lambda b, s: (b * steps + s, 0)),
                  col(COL_GQ), col(COL_GK), col(COL_GV), col(COL_GG),
                  full((D_MODEL, LANES)), full((LANES, w)), full((1, w)), full((1, GLA_DV))],
        out_specs=pl.BlockSpec((rows, w), lambda b, s: (b * steps + s, 0)),
        out_shape=jax.ShapeDtypeStruct((t, w), BF16),
        scratch_shapes=[pltpu.VMEM((GLA_HEADS, GLA_DV, GLA_DK), F32),
                        pltpu.VMEM((rows, w), F32)],
        compiler_params=_params("arbitrary", "arbitrary"),
        name="gla",
    )(xb, proj, proj, proj, proj, w_lr, w_gu, b_gate, norm_w)


def _gmlp_kernel(z_ref, lng_ref, lnb_ref, ws_ref, bs_ref, o_ref):
    rows = z_ref.shape[0]
    C = GMLP_CHUNK
    z = z_ref[...].astype(F32)
    z = 0.5 * z * (1.0 + lax.erf(z * np.float32(np.sqrt(0.5))))
    u = z[:, :GMLP_WIDTH]
    v = _layer_norm(z[:, GMLP_WIDTH:], lng_ref[...], lnb_ref[...]).astype(BF16)
    row = lax.broadcasted_iota(jnp.int32, (C, C), 0)
    col = lax.broadcasted_iota(jnp.int32, (C, C), 1)
    tril = (row >= col).astype(F32)
    for g in range(GMLP_GROUPS):
        sl = slice(g * GMLP_GROUP_CH, (g + 1) * GMLP_GROUP_CH)
        w = (ws_ref[g] * tril).astype(BF16)
        for c in range(rows // C):
            rs = slice(c * C, (c + 1) * C)
            vs = _dot(w, v[rs, sl]) + bs_ref[:, sl]
            o_ref[rs, sl] = (u[rs, sl] * vs).astype(o_ref.dtype)


def _gmlp(proj, ln_g, ln_b, w_s, bs_full, layer, *, rows):
    t = proj.shape[0]

    def full(shape):
        return pl.BlockSpec((None,) + shape, lambda i: (layer,) + (0,) * len(shape))

    return pl.pallas_call(
        _gmlp_kernel,
        grid=(t // rows,),
        in_specs=[pl.BlockSpec((rows, 2 * GMLP_WIDTH), lambda i: (i, COL_GZ // (2 * GMLP_WIDTH))),
                  full((1, GMLP_WIDTH)), full((1, GMLP_WIDTH)),
                  full((GMLP_GROUPS, GMLP_CHUNK, GMLP_CHUNK)), full((GMLP_CHUNK, GMLP_WIDTH))],
        out_specs=pl.BlockSpec((rows, GMLP_WIDTH), lambda i: (i, 0)),
        out_shape=jax.ShapeDtypeStruct((t, GMLP_WIDTH), BF16),
        compiler_params=_params("parallel"),
        name="gmlp",
    )(proj, ln_g, ln_b, w_s, bs_full)


def _rope_table_kernel(pos_ref, inv_ref, c_ref, s1_ref, s2_ref):
    half = ROPE_DIMS // 2
    ang = pos_ref[...].astype(F32) * inv_ref[...]
    lane = lax.broadcasted_iota(jnp.int32, ang.shape, 1)
    cos = jnp.cos(ang)
    sin = jnp.sin(ang)
    c_ref[...] = jnp.where(lane < ROPE_DIMS, cos, 1.0)
    s1_ref[...] = jnp.where(lane < half, -sin, 0.0)
    s2_ref[...] = jnp.where((lane >= half) & (lane < ROPE_DIMS), sin, 0.0)


def _rope_tables(pos_col, inv_row, *, rows):
    t = pos_col.shape[0]
    spec = pl.BlockSpec((rows, LANES), lambda i: (i, 0))
    shp = jax.ShapeDtypeStruct((t, LANES), F32)
    return pl.pallas_call(
        _rope_table_kernel,
        grid=(t // rows,),
        in_specs=[pl.BlockSpec((rows, 1), lambda i: (i, 0)),
                  pl.BlockSpec((1, LANES), lambda i: (0, 0))],
        out_specs=[spec, spec, spec],
        out_shape=[shp, shp, shp],
        compiler_params=_params("parallel"),
        name="rope_tables",
    )(pos_col, inv_row)


def _moba_prep_kernel(q_ref, k_ref, v_ref, c_ref, s1_ref, s2_ref, qt_ref, ko_ref, vt_ref, km_ref):
    half = ROPE_DIMS // 2
    q_scale = np.float32(MOBA_HD ** -0.5 * np.log2(np.e))

    for n in range(q_ref.shape[0] // MOBA_BLOCK):
        rs = slice(n * MOBA_BLOCK, (n + 1) * MOBA_BLOCK)
        cos = c_ref[rs, :]
        s1 = s1_ref[rs, :]
        s2 = s2_ref[rs, :]

        def rope(t):
            return (t * cos + pltpu.roll(t, LANES - half, 1) * s1 + pltpu.roll(t, half, 1) * s2)

        for h in range(MOBA_HEADS):
            sl = slice(h * LANES, (h + 1) * LANES)
            qr = rope(q_ref[rs, sl].astype(F32)) * q_scale
            qt_ref[sl, rs] = qr.T.astype(qt_ref.dtype)
            kr = rope(k_ref[rs, sl].astype(F32))
            ko_ref[rs, sl] = kr.astype(ko_ref.dtype)
            km_ref[n, :, sl] = jnp.mean(kr, axis=0, keepdims=True)
            vt_ref[h * MOBA_VT_ROWS:h * MOBA_VT_ROWS + MOBA_HD, rs] = (
                v_ref[rs, sl].astype(F32).T.astype(vt_ref.dtype))
            vt_ref[h * MOBA_VT_ROWS + MOBA_HD:(h + 1) * MOBA_VT_ROWS, rs] = jnp.ones(
                (MOBA_VT_ROWS - MOBA_HD, MOBA_BLOCK), vt_ref.dtype)


def _moba_prep(proj3, cos3, s13, s23, *, rows):
    batch, seq, _ = proj3.shape
    w = MOBA_HEADS * MOBA_HD
    nb = seq // MOBA_BLOCK
    tab = pl.BlockSpec((None, rows, LANES), lambda b, i: (b, i, 0))

    def col(off):
        return pl.BlockSpec((None, rows, w), lambda b, i: (b, i, off // w))

    vw = MOBA_HEADS * MOBA_VT_ROWS
    return pl.pallas_call(
        _moba_prep_kernel,
        grid=(batch, seq // rows),
        in_specs=[col(COL_MQ), col(COL_MK), col(COL_MV), tab, tab, tab],
        out_specs=[pl.BlockSpec((None, w, rows), lambda b, i: (b, 0, i)),
                   pl.BlockSpec((None, rows, w), lambda b, i: (b, i, 0)),
                   pl.BlockSpec((None, vw, rows), lambda b, i: (b, 0, i)),
                   pl.BlockSpec((None, rows // MOBA_BLOCK, 1, w), lambda b, i: (b, i, 0, 0))],
        out_shape=[jax.ShapeDtypeStruct((batch, w, seq), BF16),
                   jax.ShapeDtypeStruct((batch, seq, w), BF16),
                   jax.ShapeDtypeStruct((batch, vw, seq), BF16),
                   jax.ShapeDtypeStruct((batch, nb, 1, w), F32)],
        compiler_params=_params("parallel", "parallel"),
        name="moba_prep",
    )(proj3, proj3, proj3, cos3, s13, s23)


def _moba_flash_kernel(qt_ref, k_ref, vt_ref, km_ref, o_ref, bias_ref, s_ref, acc_ref, *,
                       heads, tiles):
    BLK = MOBA_BLOCK
    first = pl.program_id(2) * tiles
    n_blk = km_ref.shape[1]
    streams = [(b, h, t) for b in range(qt_ref.shape[0]) for h in range(heads)
               for t in range(tiles)]
    blk = lax.broadcasted_iota(jnp.int32, (n_blk, BLK), 0)
    blk_f = blk.astype(F32)
    key = lax.broadcasted_iota(jnp.int32, (BLK, BLK), 0)
    qry = lax.broadcasted_iota(jnp.int32, (BLK, BLK), 1)
    hs = [slice(h * MOBA_HD, (h + 1) * MOBA_HD) for h in range(heads)]
    vs = [slice(h * MOBA_VT_ROWS, (h + 1) * MOBA_VT_ROWS) for h in range(heads)]
    ts = [slice(t * BLK, (t + 1) * BLK) for t in range(tiles)]

    def scores(b, h, t, start):
        return _dot(k_ref[b, pl.ds(start, BLK), hs[h]], qt_ref[b, hs[h], ts[t]])

    def past_block(n, b, h, m, s_max, j):
        bias = bias_ref[n, pl.ds(j, 1), :]
        m_new = jnp.maximum(m, s_max + bias)
        m_use = jnp.where(m_new == -jnp.inf, 0.0, m_new)
        alpha = jnp.exp2(m - m_use)
        p = jnp.exp2(s_ref[n] - (m_use - bias))
        c0 = pl.multiple_of(j * BLK, BLK)
        acc_ref[n] = alpha * acc_ref[n] + _dot(vt_ref[b, vs[h], pl.ds(c0, BLK)], p.astype(BF16))
        return m_new

    carry0 = []
    for n, (b, h, t) in enumerate(streams):
        own = first + t
        qt = qt_ref[b, hs[h], ts[t]]
        km = km_ref[b, :, hs[h]]
        km_hi = km.astype(BF16)
        km_lo = (km - km_hi.astype(F32)).astype(BF16)
        gate = _dot(km_hi, qt) + _dot(km_lo, qt)
        cand = blk < own
        sel = jnp.zeros(gate.shape, jnp.bool_)
        for r in range(min(MOBA_TOPK, n_blk)):
            g_eff = jnp.where(cand, gate, -jnp.inf)
            best = jnp.max(g_eff, axis=0, keepdims=True)
            idx = jnp.min(jnp.where(cand & (g_eff == best), blk_f, float(n_blk)),
                          axis=0, keepdims=True)
            idx = jnp.where(r < own, idx, float(n_blk))
            sel = sel | (blk_f == idx)
            cand = cand & (blk_f != idx)
        bias_ref[n] = jnp.where(sel, 0.0, -jnp.inf)
        s0 = scores(b, h, t, 0)
        s_ref[n] = s0
        acc_ref[n] = jnp.zeros((MOBA_VT_ROWS, BLK), F32)
        carry0 += [jnp.full((1, BLK), -jnp.inf, F32), jnp.max(s0, axis=0, keepdims=True)]

    def body(j, carry):
        n0 = pl.multiple_of((j + 1) * BLK, BLK)
        out = []
        for n, (b, h, t) in enumerate(streams):
            m, s_max = carry[2 * n:2 * n + 2]
            s_next = scores(b, h, t, n0)
            m_new = past_block(n, b, h, m, s_max, j)
            s_ref[n] = s_next
            out += [m_new, jnp.max(s_next, axis=0, keepdims=True)]
        return tuple(out)

    carry = list(lax.fori_loop(0, first, body, tuple(carry0)))
    for u in range(tiles):
        for n, (b, h, t) in enumerate(streams):
            m, s_max = carry[2 * n:2 * n + 2]
            if t == u:
                s = jnp.where(key <= qry, s_ref[n], -jnp.inf)
                m_new = jnp.maximum(m, jnp.max(s, axis=0, keepdims=True))
                alpha = jnp.exp2(m - m_new)
                p = jnp.exp2(s - m_new)
                r0 = pl.multiple_of((first + u) * BLK, BLK)
                acc = alpha * acc_ref[n] + _dot(vt_ref[b, vs[h], pl.ds(r0, BLK)], p.astype(BF16))
                o_ref[b, ts[t], hs[h]] = (
                    acc[:MOBA_HD] / acc[MOBA_HD:MOBA_HD + 1]).T.astype(o_ref.dtype)
            elif t > u:
                carry[2 * n] = past_block(n, b, h, m, s_max, first + u)
                s_next = scores(b, h, t, pl.multiple_of((first + u + 1) * BLK, BLK))
                s_ref[n] = s_next
                carry[2 * n + 1] = jnp.max(s_next, axis=0, keepdims=True)


def _moba_flash(q_t, k_rot, v_t, k_mean, *, rows, heads, tiles):
    batch, seq, w = k_rot.shape
    n_blk = seq // MOBA_BLOCK
    hw = heads * MOBA_HD
    tq = tiles * MOBA_BLOCK
    n_streams = rows * heads * tiles
    once = pl.Buffered(1)
    return pl.pallas_call(
        functools.partial(_moba_flash_kernel, heads=heads, tiles=tiles),
        grid=(batch // rows, MOBA_HEADS // heads, seq // tq),
        in_specs=[pl.BlockSpec((rows, hw, tq), lambda b, g, i: (b, g, i)),
                  pl.BlockSpec((rows, seq, hw), lambda b, g, i: (b, 0, g), pipeline_mode=once),
                  pl.BlockSpec((rows, heads * MOBA_VT_ROWS, seq), lambda b, g, i: (b, g, 0),
                               pipeline_mode=once),
                  pl.BlockSpec((rows, n_blk, hw), lambda b, g, i: (b, 0, g))],
        out_specs=pl.BlockSpec((rows, tq, hw), lambda b, g, i: (b, i, g)),
        out_shape=jax.ShapeDtypeStruct((batch, seq, w), BF16),
        scratch_shapes=[pltpu.VMEM((n_streams, n_blk, MOBA_BLOCK), F32),
                        pltpu.VMEM((n_streams, MOBA_BLOCK, MOBA_BLOCK), F32),
                        pltpu.VMEM((n_streams, MOBA_VT_ROWS, MOBA_BLOCK), F32)],
        compiler_params=_params("parallel", "parallel", "arbitrary"),
        name="moba_flash",
    )(q_t, k_rot, v_t, k_mean)


def _merge_kernel(a_ref, b_ref, c_ref, gl_ref, x_ref, wb_ref, wo_ref, g_ref, bt_ref,
                  xo_ref, xb_ref):
    gl = gl_ref[...].astype(F32)
    d = D_MODEL
    m = (jax.nn.sigmoid(gl[:, :d]) * _dot(a_ref[...], wb_ref[0])
         + jax.nn.sigmoid(gl[:, d:2 * d]) * _dot(b_ref[...], wb_ref[1])
         + jax.nn.sigmoid(gl[:, 2 * d:]) * _dot(c_ref[...], wb_ref[2]))
    mix = _dot(m.astype(BF16), wo_ref[...])
    y = _layer_norm(DEEPNORM_ALPHA * x_ref[...] + mix, g_ref[...], bt_ref[...])
    xo_ref[...] = y
    xb_ref[...] = y.astype(BF16)


def _merge(a, b, c, proj, x, w_branch, w_out, ln_g, ln_b, layer, *, tm):
    t = x.shape[0]
    br = pl.BlockSpec((tm, BRANCH_W), lambda i: (i, 0))
    xs = pl.BlockSpec((tm, D_MODEL), lambda i: (i, 0))

    def full(shape):
        return pl.BlockSpec((None,) + shape, lambda i: (layer,) + (0,) * len(shape))

    return pl.pallas_call(
        _merge_kernel,
        grid=(t // tm,),
        in_specs=[br, br, br,
                  pl.BlockSpec((tm, 3 * D_MODEL), lambda i: (i, COL_GL // (3 * D_MODEL))),
                  xs, full((3, BRANCH_W, D_MODEL)), full((D_MODEL, D_MODEL)),
                  full((1, D_MODEL)), full((1, D_MODEL))],
        out_specs=[xs, xs],
        out_shape=[jax.ShapeDtypeStruct((t, D_MODEL), F32),
                   jax.ShapeDtypeStruct((t, D_MODEL), BF16)],
        compiler_params=_params("parallel"),
        name="merge_ln1",
    )(a, b, c, proj, x, w_branch, w_out, ln_g, ln_b)


def _ffn_kernel(xb_ref, x_ref, w1_ref, w2_ref, g_ref, bt_ref, xo_ref, xbo_ref):
    h = jnp.square(jnp.maximum(_dot(xb_ref[...], w1_ref[...]), 0.0)).astype(BF16)
    ff = _dot(h, w2_ref[...])
    y = _layer_norm(DEEPNORM_ALPHA * x_ref[...] + ff, g_ref[...], bt_ref[...])
    xo_ref[...] = y
    xbo_ref[...] = y.astype(BF16)


def _ffn(xb, x, w1, w2, ln_g, ln_b, layer, *, tm):
    t = x.shape[0]
    xs = pl.BlockSpec((tm, D_MODEL), lambda i: (i, 0))
    vec = pl.BlockSpec((None, 1, D_MODEL), lambda i: (layer, 0, 0))
    once = pl.Buffered(1)
    return pl.pallas_call(
        _ffn_kernel,
        grid=(t // tm,),
        in_specs=[xs, xs,
                  pl.BlockSpec((None, D_MODEL, D_FF), lambda i: (layer, 0, 0), pipeline_mode=once),
                  pl.BlockSpec((None, D_FF, D_MODEL), lambda i: (layer, 0, 0), pipeline_mode=once),
                  vec, vec],
        out_specs=[xs, xs],
        out_shape=[jax.ShapeDtypeStruct((t, D_MODEL), F32),
                   jax.ShapeDtypeStruct((t, D_MODEL), BF16)],
        compiler_params=_params("parallel"),
        name="ffn_ln2",
    )(xb, x, w1, w2, ln_g, ln_b)


REORDER_W = 512
N_GL, N_G4, N_REST = 6, 4, 5
NAT_REST = 2048
NAT_GL = NAT_REST + N_REST * REORDER_W


def _reorder_kernel(a_ref, b_ref, o_ref):
    shifted = (pl.program_id(1) < N_GL) | (pl.program_id(1) >= N_GL + N_G4)

    @pl.when(shifted)
    def _():
        a = a_ref[...]
        o_ref[...] = jnp.concatenate(
            [a[:, GLA_GATE_RANK:], b_ref[:, :GLA_GATE_RANK]], axis=1).astype(o_ref.dtype)

    @pl.when(jnp.logical_not(shifted))
    def _():
        o_ref[...] = a_ref[...].astype(o_ref.dtype)


def _split_w_in(w_in):
    depth, d, _ = w_in.shape
    w = REORDER_W

    def a_idx(j):
        return jnp.where(j < N_GL, NAT_GL // w + j,
                         jnp.where(j < N_GL + N_G4, j - N_GL, NAT_REST // w + j - N_GL - N_G4))

    def b_idx(j):
        return (a_idx(j) + 1) * (w // LANES)

    w_main = pl.pallas_call(
        _reorder_kernel,
        grid=(depth, N_GL + N_G4 + N_REST),
        in_specs=[pl.BlockSpec((None, d, w), lambda l, j: (l, 0, a_idx(j))),
                  pl.BlockSpec((None, d, LANES), lambda l, j: (l, 0, b_idx(j)))],
        out_specs=pl.BlockSpec((None, d, w), lambda l, j: (l, 0, j)),
        out_shape=jax.ShapeDtypeStruct((depth, d, PROJ_COLS), BF16),
        compiler_params=_params("parallel", "parallel"),
        name="w_in_reorder",
    )(w_in, w_in)
    glr = w_in[..., NAT_REST:NAT_REST + GLA_GATE_RANK]
    w_lr = jnp.pad(glr, ((0, 0), (0, 0), (0, LANES - GLA_GATE_RANK))).astype(BF16)
    return w_main, w_lr


def kernel(x, positions, w_in, w_gate_up, b_gate, gla_norm_w, gmlp_ln_g, gmlp_ln_b, gmlp_w_s,
           gmlp_b_s, w_branch, w_out, ln1_g, ln1_b, w_ff1, w_ff2, ln2_g, ln2_b):
    batch, seq, d = x.shape
    t = batch * seq
    depth = w_in.shape[0]
    assert d == D_MODEL and seq % 512 == 0

    half = ROPE_DIMS // 2
    inv = 1.0 / (ROPE_THETA ** (jnp.arange(half, dtype=F32) * (2.0 / ROPE_DIMS)))
    inv_row = jnp.concatenate([inv, inv, jnp.zeros((LANES - ROPE_DIMS,), F32)])[None, :]
    cos, s1, s2 = [a.reshape(batch, seq, LANES) for a in
                   _rope_tables(positions.reshape(t, 1), inv_row, rows=min(1024, seq))]

    w_main, w_lr = _split_w_in(w_in)
    w_gu = jnp.pad(w_gate_up, ((0, 0), (0, LANES - GLA_GATE_RANK), (0, 0))).astype(BF16)
    bs_full = jnp.repeat(jnp.swapaxes(gmlp_b_s, 1, 2), GMLP_GROUP_CH, axis=2)
    w_branch_b, w_out_b = w_branch.astype(BF16), w_out.astype(BF16)
    w_ff1_b, w_ff2_b = w_ff1.astype(BF16), w_ff2.astype(BF16)
    row = lambda v: v[:, None, :]

    xf = x.reshape(t, d)
    xb = xf.astype(BF16)
    for l in range(depth):
        proj = _matmul(xb, w_main, l, tm=2048, tn=1536, out_dtype=BF16)
        a = _gla(xb, proj, w_lr, w_gu, row(b_gate), row(gla_norm_w), l, batch=batch, rows=1024)
        b = _gmlp(proj, row(gmlp_ln_g), row(gmlp_ln_b), gmlp_w_s, bs_full, l, rows=512)
        q_t, k_rot, v_t, k_mean = _moba_prep(proj.reshape(batch, seq, -1), cos, s1, s2,
                                             rows=min(1024, seq))
        c = _moba_flash(q_t, k_rot, v_t, k_mean.reshape(batch, seq // MOBA_BLOCK, -1),
                        rows=2 if batch % 2 == 0 else 1, heads=MOBA_HEADS, tiles=2)
        xf, xb = _merge(a, b, c.reshape(t, -1), proj, xf, w_branch_b, w_out_b,
                        row(ln1_g), row(ln1_b), l, tm=512)
        xf, xb = _ffn(xb, xf, w_ff1_b, w_ff2_b, row(ln2_g), row(ln2_b), l, tm=512)
    return xf.reshape(batch, seq, d)
```

```python
import functools

import numpy as np
import jax
import jax.numpy as jnp
from jax import lax
from jax.experimental import pallas as pl
from jax.experimental.pallas import tpu as pltpu

D_MODEL = 1024
DEPTH = 4
GLA_HEADS = 4
GLA_DK = 128
GLA_DV = 128
GLA_GATE_RANK = 16
GLA_GATE_NORM = 16.0
GLA_CHUNK = 64
GMLP_GROUPS = 4
GMLP_GROUP_CH = 128
GMLP_WIDTH = 512
GMLP_CHUNK = 128
MOBA_HEADS = 4
MOBA_HD = 128
MOBA_BLOCK = 256
MOBA_TOPK = 3
MOBA_VT_ROWS = MOBA_HD + 16
ROPE_THETA = 500000.0
ROPE_DIMS = MOBA_HD // 4
BRANCH_W = 512
D_FF = 4 * D_MODEL
DEEPNORM_ALPHA = (2 * DEPTH) ** 0.25
LN_EPS = 1e-5
RMS_EPS = 1e-6

LANES = 128
VMEM_LIMIT = 56 * 1024 * 1024

COL_GL = 0
COL_GQ = 3072
COL_GK = 3584
COL_GV = 4096
COL_GG = 4608
COL_GZ = 5120
COL_MQ = 6144
COL_MK = 6656
COL_MV = 7168
PROJ_COLS = 7680

BF16 = jnp.bfloat16
F32 = jnp.float32


def _dot(a, b):
    return jnp.dot(a, b, preferred_element_type=F32)


def _dot_nt(a, b):
    return lax.dot_general(a, b, (((1,), (1,)), ((), ())), preferred_element_type=F32)


def _dot_tn(a, b):
    return lax.dot_general(a, b, (((0,), (0,)), ((), ())), preferred_element_type=F32)


def _params(*sem):
    return pltpu.CompilerParams(dimension_semantics=sem, vmem_limit_bytes=VMEM_LIMIT)


def _layer_norm(y, g, b):
    mu = jnp.mean(y, axis=-1, keepdims=True)
    yc = y - mu
    var = jnp.mean(yc * yc, axis=-1, keepdims=True)
    return yc * lax.rsqrt(var + LN_EPS) * g + b


def _gelu(z):
    return 0.5 * z * (1.0 + lax.erf(z * np.float32(np.sqrt(0.5))))


def _matmul_kernel(x_ref, w_ref, lng_ref, lnb_ref, o_ref, *, gz_block, gz_lane):
    j = pl.program_id(1)

    @pl.when(j != gz_block)
    def _():
        o_ref[...] = _dot(x_ref[...], w_ref[...]).astype(o_ref.dtype)

    @pl.when(j == gz_block)
    def _():
        acc = _dot(x_ref[...], w_ref[...])
        z = _gelu(acc[:, gz_lane:])
        v = _layer_norm(z[:, GMLP_WIDTH:], lng_ref[...], lnb_ref[...])
        o_ref[:, :gz_lane] = acc[:, :gz_lane].astype(o_ref.dtype)
        o_ref[:, gz_lane:gz_lane + GMLP_WIDTH] = z[:, :GMLP_WIDTH].astype(o_ref.dtype)
        o_ref[:, gz_lane + GMLP_WIDTH:] = v.astype(o_ref.dtype)


def _matmul(x, w, ln_g, ln_b, layer, *, tm, tn, out_dtype):
    m, k = x.shape
    n = w.shape[2]
    gz_block, gz_lane = divmod(COL_GZ, tn)
    assert gz_lane + 2 * GMLP_WIDTH == tn
    vec = pl.BlockSpec((None, 1, GMLP_WIDTH), lambda i, j: (layer, 0, 0))
    return pl.pallas_call(
        functools.partial(_matmul_kernel, gz_block=gz_block, gz_lane=gz_lane),
        grid=(m // tm, n // tn),
        in_specs=[pl.BlockSpec((tm, k), lambda i, j: (i, 0)),
                  pl.BlockSpec((None, k, tn), lambda i, j: (layer, 0, j)),
                  vec, vec],
        out_specs=pl.BlockSpec((tm, tn), lambda i, j: (i, j)),
        out_shape=jax.ShapeDtypeStruct((m, n), out_dtype),
        compiler_params=_params("parallel", "parallel"),
        name="in_proj",
    )(x, w, ln_g, ln_b)


def _gla_kernel(x_ref, q_ref, k_ref, v_ref, g_ref, wlr_ref, wgu_ref, bg_ref, nw_ref,
                o_ref, st_ref, la_ref):
    rows = x_ref.shape[0]
    C = GLA_CHUNK

    @pl.when(pl.program_id(1) == 0)
    def _():
        st_ref[...] = jnp.zeros_like(st_ref)

    lr = _dot(x_ref[...], wlr_ref[...]).astype(BF16)
    z = _dot(lr, wgu_ref[...]) + bg_ref[...]
    la_ref[...] = (jnp.minimum(z, 0.0) - jnp.log1p(jnp.exp(-jnp.abs(z)))) / GLA_GATE_NORM

    row = lax.broadcasted_iota(jnp.int32, (C, C), 0)
    col = lax.broadcasted_iota(jnp.int32, (C, C), 1)
    causal = row >= col
    tril = causal.astype(BF16)
    nw = nw_ref[...]

    nc = rows // C
    hs = [slice(h * LANES, (h + 1) * LANES) for h in range(GLA_HEADS)]
    rs = [slice(c * C, (c + 1) * C) for c in range(nc)]

    q_dec, k_inv, k_end, decay, vv = [], [], [], [], []
    for c in range(nc):
        la = la_ref[rs[c], :]
        hi = la.astype(BF16)
        r1 = la - hi.astype(F32)
        mid = r1.astype(BF16)
        lo = (r1 - mid.astype(F32)).astype(BF16)
        bcum = _dot(tril, hi) + _dot(tril, mid) + _dot(tril, lo)
        b_end = bcum[C - 1:C, :]
        q = q_ref[rs[c], :].astype(F32) * (GLA_DK ** -0.5)
        k = k_ref[rs[c], :].astype(F32)
        q_dec.append((q * jnp.exp(bcum)).astype(BF16))
        k_inv.append((k * jnp.exp(-bcum)).astype(BF16))
        k_end.append((k * jnp.exp(b_end - bcum)).astype(BF16))
        decay.append(jnp.exp(b_end))
        vv.append(v_ref[rs[c], :].astype(BF16))

    attn = [[jnp.where(causal, _dot_nt(q_dec[c][:, sl], k_inv[c][:, sl]), 0.0).astype(BF16)
             for sl in hs] for c in range(nc)]
    kv = [[_dot_tn(vv[c][:, sl], k_end[c][:, sl]) for sl in hs] for c in range(nc)]
    o_intra = [[_dot(attn[c][h], vv[c][:, hs[h]]) for h in range(GLA_HEADS)] for c in range(nc)]

    st = [st_ref[h] for h in range(GLA_HEADS)]
    for c in range(nc):
        g = g_ref[rs[c], :].astype(F32)
        for h, sl in enumerate(hs):
            o = o_intra[c][h] + _dot_nt(q_dec[c][:, sl], st[h].astype(BF16))
            st[h] = decay[c][:, sl] * st[h] + kv[c][h]
            o = o * lax.rsqrt(jnp.mean(o * o, axis=-1, keepdims=True) + RMS_EPS) * nw
            gh = g[:, sl]
            o = o * (gh * jax.nn.sigmoid(gh))
            o_ref[rs[c], sl] = o.astype(o_ref.dtype)
    for h in range(GLA_HEADS):
        st_ref[h] = st[h]


def _gla(xb, proj, w_lr, w_gu, b_gate, norm_w, layer, *, batch, rows):
    t = xb.shape[0]
    steps = t // batch // rows
    w = GLA_HEADS * GLA_DK

    def col(off):
        return pl.BlockSpec((rows, w), lambda b, s: (b * steps + s, off // w))

    def full(shape):
        return pl.BlockSpec((None,) + shape, lambda b, s: (layer,) + (0,) * len(shape))

    return pl.pallas_call(
        _gla_kernel,
        grid=(batch, steps),
        in_specs=[pl.BlockSpec((rows, D_MODEL), lambda b, s: (b * steps + s, 0)),
                  col(COL_GQ), col(COL_GK), col(COL_GV), col(COL_GG),
                  full((D_MODEL, LANES)), full((LANES, w)), full((1, w)), full((1, GLA_DV))],
        out_specs=pl.BlockSpec((rows, w), lambda b, s: (b * steps + s, 0)),
        out_shape=jax.ShapeDtypeStruct((t, w), BF16),
        scratch_shapes=[pltpu.VMEM((GLA_HEADS, GLA_DV, GLA_DK), F32),
                        pltpu.VMEM((rows, w), F32)],
        compiler_params=_params("arbitrary", "arbitrary"),
        name="gla",
    )(xb, proj, proj, proj, proj, w_lr, w_gu, b_gate, norm_w)


def _gmlp_kernel(z_ref, ws_ref, bs_ref, o_ref):
    rows = z_ref.shape[0]
    C = GMLP_CHUNK
    u = z_ref[:, :GMLP_WIDTH].astype(F32)
    v = z_ref[:, GMLP_WIDTH:]
    row = lax.broadcasted_iota(jnp.int32, (C, C), 0)
    col = lax.broadcasted_iota(jnp.int32, (C, C), 1)
    tril = (row >= col).astype(F32)
    for g in range(GMLP_GROUPS):
        sl = slice(g * GMLP_GROUP_CH, (g + 1) * GMLP_GROUP_CH)
        w = (ws_ref[g] * tril).astype(BF16)
        for c in range(rows // C):
            rs = slice(c * C, (c + 1) * C)
            vs = _dot(w, v[rs, sl]) + bs_ref[:, sl]
            o_ref[rs, sl] = (u[rs, sl] * vs).astype(o_ref.dtype)


def _gmlp(proj, w_s, bs_full, layer, *, rows):
    t = proj.shape[0]

    def full(shape):
        return pl.BlockSpec((None,) + shape, lambda i: (layer,) + (0,) * len(shape))

    return pl.pallas_call(
        _gmlp_kernel,
        grid=(t // rows,),
        in_specs=[pl.BlockSpec((rows, 2 * GMLP_WIDTH), lambda i: (i, COL_GZ // (2 * GMLP_WIDTH))),
                  full((GMLP_GROUPS, GMLP_CHUNK, GMLP_CHUNK)), full((GMLP_CHUNK, GMLP_WIDTH))],
        out_specs=pl.BlockSpec((rows, GMLP_WIDTH), lambda i: (i, 0)),
        out_shape=jax.ShapeDtypeStruct((t, GMLP_WIDTH), BF16),
        compiler_params=_params("parallel"),
        name="gmlp",
    )(proj, w_s, bs_full)


def _rope_table_kernel(pos_ref, inv_ref, c_ref, s1_ref, s2_ref):
    half = ROPE_DIMS // 2
    ang = pos_ref[...].astype(F32) * inv_ref[...]
    lane = lax.broadcasted_iota(jnp.int32, ang.shape, 1)
    cos = jnp.cos(ang)
    sin = jnp.sin(ang)
    c_ref[...] = jnp.where(lane < ROPE_DIMS, cos, 1.0)
    s1_ref[...] = jnp.where(lane < half, -sin, 0.0)
    s2_ref[...] = jnp.where((lane >= half) & (lane < ROPE_DIMS), sin, 0.0)


def _rope_tables(pos_col, inv_row, *, rows):
    t = pos_col.shape[0]
    spec = pl.BlockSpec((rows, LANES), lambda i: (i, 0))
    shp = jax.ShapeDtypeStruct((t, LANES), F32)
    return pl.pallas_call(
        _rope_table_kernel,
        grid=(t // rows,),
        in_specs=[pl.BlockSpec((rows, 1), lambda i: (i, 0)),
                  pl.BlockSpec((1, LANES), lambda i: (0, 0))],
        out_specs=[spec, spec, spec],
        out_shape=[shp, shp, shp],
        compiler_params=_params("parallel"),
        name="rope_tables",
    )(pos_col, inv_row)


def _moba_prep_kernel(q_ref, k_ref, v_ref, c_ref, s1_ref, s2_ref, qt_ref, ko_ref, vt_ref, km_ref):
    half = ROPE_DIMS // 2
    q_scale = np.float32(MOBA_HD ** -0.5 * np.log2(np.e))

    for n in range(q_ref.shape[0] // MOBA_BLOCK):
        rs = slice(n * MOBA_BLOCK, (n + 1) * MOBA_BLOCK)
        cos = c_ref[rs, :]
        s1 = s1_ref[rs, :]
        s2 = s2_ref[rs, :]

        def rope(t):
            return (t * cos + pltpu.roll(t, LANES - half, 1) * s1 + pltpu.roll(t, half, 1) * s2)

        for h in range(MOBA_HEADS):
            sl = slice(h * LANES, (h + 1) * LANES)
            qr = rope(q_ref[rs, sl].astype(F32)) * q_scale
            qt_ref[sl, rs] = qr.T.astype(qt_ref.dtype)
            kr = rope(k_ref[rs, sl].astype(F32))
            ko_ref[rs, sl] = kr.astype(ko_ref.dtype)
            km_ref[n, :, sl] = jnp.mean(kr, axis=0, keepdims=True)
            vt_ref[h * MOBA_VT_ROWS:h * MOBA_VT_ROWS + MOBA_HD, rs] = (
                v_ref[rs, sl].astype(F32).T.astype(vt_ref.dtype))
            vt_ref[h * MOBA_VT_ROWS + MOBA_HD:(h + 1) * MOBA_VT_ROWS, rs] = jnp.ones(
                (MOBA_VT_ROWS - MOBA_HD, MOBA_BLOCK), vt_ref.dtype)


def _moba_prep(proj3, cos3, s13, s23, *, rows):
    batch, seq, _ = proj3.shape
    w = MOBA_HEADS * MOBA_HD
    nb = seq // MOBA_BLOCK
    tab = pl.BlockSpec((None, rows, LANES), lambda b, i: (b, i, 0))

    def col(off):
        return pl.BlockSpec((None, rows, w), lambda b, i: (b, i, off // w))

    vw = MOBA_HEADS * MOBA_VT_ROWS
    return pl.pallas_call(
        _moba_prep_kernel,
        grid=(batch, seq // rows),
        in_specs=[col(COL_MQ), col(COL_MK), col(COL_MV), tab, tab, tab],
        out_specs=[pl.BlockSpec((None, w, rows), lambda b, i: (b, 0, i)),
                   pl.BlockSpec((None, rows, w), lambda b, i: (b, i, 0)),
                   pl.BlockSpec((None, vw, rows), lambda b, i: (b, 0, i)),
                   pl.BlockSpec((None, rows // MOBA_BLOCK, 1, w), lambda b, i: (b, i, 0, 0))],
        out_shape=[jax.ShapeDtypeStruct((batch, w, seq), BF16),
                   jax.ShapeDtypeStruct((batch, seq, w), BF16),
                   jax.ShapeDtypeStruct((batch, vw, seq), BF16),
                   jax.ShapeDtypeStruct((batch, nb, 1, w), F32)],
        compiler_params=_params("parallel", "parallel"),
        name="moba_prep",
    )(proj3, proj3, proj3, cos3, s13, s23)


def _moba_flash_kernel(qt_ref, k_ref, vt_ref, km_ref, o_ref, bias_ref, s_ref, acc_ref, *,
                       heads, tiles):
    BLK = MOBA_BLOCK
    first = pl.program_id(2) * tiles
    n_blk = km_ref.shape[1]
    streams = [(b, h, t) for b in range(qt_ref.shape[0]) for h in range(heads)
               for t in range(tiles)]
    blk = lax.broadcasted_iota(jnp.int32, (n_blk, BLK), 0)
    blk_f = blk.astype(F32)
    key = lax.broadcasted_iota(jnp.int32, (BLK, BLK), 0)
    qry = lax.broadcasted_iota(jnp.int32, (BLK, BLK), 1)
    hs = [slice(h * MOBA_HD, (h + 1) * MOBA_HD) for h in range(heads)]
    vs = [slice(h * MOBA_VT_ROWS, (h + 1) * MOBA_VT_ROWS) for h in range(heads)]
    ts = [slice(t * BLK, (t + 1) * BLK) for t in range(tiles)]

    def scores(b, h, t, start):
        return _dot(k_ref[b, pl.ds(start, BLK), hs[h]], qt_ref[b, hs[h], ts[t]])

    def past_block(n, b, h, m, s_max, j):
        bias = bias_ref[n, pl.ds(j, 1), :]
        m_new = jnp.maximum(m, s_max + bias)
        m_use = jnp.where(m_new == -jnp.inf, 0.0, m_new)
        alpha = jnp.exp2(m - m_use)
        p = jnp.exp2(s_ref[n] - (m_use - bias))
        c0 = pl.multiple_of(j * BLK, BLK)
        acc_ref[n] = alpha * acc_ref[n] + _dot(vt_ref[b, vs[h], pl.ds(c0, BLK)], p.astype(BF16))
        return m_new

    carry0 = []
    for n, (b, h, t) in enumerate(streams):
        own = first + t
        qt = qt_ref[b, hs[h], ts[t]]
        km = km_ref[b, :, hs[h]]
        km_hi = km.astype(BF16)
        km_lo = (km - km_hi.astype(F32)).astype(BF16)
        gate = _dot(km_hi, qt) + _dot(km_lo, qt)
        cand = blk < own
        sel = jnp.zeros(gate.shape, jnp.bool_)
        for r in range(min(MOBA_TOPK, n_blk)):
            g_eff = jnp.where(cand, gate, -jnp.inf)
            best = jnp.max(g_eff, axis=0, keepdims=True)
            idx = jnp.min(jnp.where(cand & (g_eff == best), blk_f, float(n_blk)),
                          axis=0, keepdims=True)
            idx = jnp.where(r < own, idx, float(n_blk))
            sel = sel | (blk_f == idx)
            cand = cand & (blk_f != idx)
        bias_ref[n] = jnp.where(sel, 0.0, -jnp.inf)
        s0 = scores(b, h, t, 0)
        s_ref[n] = s0
        acc_ref[n] = jnp.zeros((MOBA_VT_ROWS, BLK), F32)
        carry0 += [jnp.full((1, BLK), -jnp.inf, F32), jnp.max(s0, axis=0, keepdims=True)]

    def body(j, carry):
        n0 = pl.multiple_of((j + 1) * BLK, BLK)
        out = []
        for n, (b, h, t) in enumerate(streams):
            m, s_max = carry[2 * n:2 * n + 2]
            s_next = scores(b, h, t, n0)
            m_new = past_block(n, b, h, m, s_max, j)
            s_ref[n] = s_next
            out += [m_new, jnp.max(s_next, axis=0, keepdims=True)]
        return tuple(out)

    carry = list(lax.fori_loop(0, first, body, tuple(carry0)))
    for u in range(tiles):
        for n, (b, h, t) in enumerate(streams):
            m, s_max = carry[2 * n:2 * n + 2]
            if t == u:
                s = jnp.where(key <= qry, s_ref[n], -jnp.inf)
                m_new = jnp.maximum(m, jnp.max(s, axis=0, keepdims=True))
                alpha = jnp.exp2(m - m_new)
                p = jnp.exp2(s - m_new)
                r0 = pl.multiple_of((first + u) * BLK, BLK)
                acc = alpha * acc_ref[n] + _dot(vt_ref[b, vs[h], pl.ds(r0, BLK)], p.astype(BF16))
                o_ref[b, ts[t], hs[h]] = (
                    acc[:MOBA_HD] / acc[MOBA_HD:MOBA_HD + 1]).T.astype(o_ref.dtype)
            elif t > u:
                carry[2 * n] = past_block(n, b, h, m, s_max, first + u)
                s_next = scores(b, h, t, pl.multiple_of((first + u + 1) * BLK, BLK))
                s_ref[n] = s_next
                carry[2 * n + 1] = jnp.max(s_next, axis=0, keepdims=True)


def _moba_flash(q_t, k_rot, v_t, k_mean, *, rows, heads, tiles):
    batch, seq, w = k_rot.shape
    n_blk = seq // MOBA_BLOCK
    hw = heads * MOBA_HD
    tq = tiles * MOBA_BLOCK
    n_streams = rows * heads * tiles
    once = pl.Buffered(1)
    return pl.pallas_call(
        functools.partial(_moba_flash_kernel, heads=heads, tiles=tiles),
        grid=(batch // rows, MOBA_HEADS // heads, seq // tq),
        in_specs=[pl.BlockSpec((rows, hw, tq), lambda b, g, i: (b, g, i)),
                  pl.BlockSpec((rows, seq, hw), lambda b, g, i: (b, 0, g), pipeline_mode=once),
                  pl.BlockSpec((rows, heads * MOBA_VT_ROWS, seq), lambda b, g, i: (b, g, 0),
                               pipeline_mode=once),
                  pl.BlockSpec((rows, n_blk, hw), lambda b, g, i: (b, 0, g))],
        out_specs=pl.BlockSpec((rows, tq, hw), lambda b, g, i: (b, i, g)),
        out_shape=jax.ShapeDtypeStruct((batch, seq, w), BF16),
        scratch_shapes=[pltpu.VMEM((n_streams, n_blk, MOBA_BLOCK), F32),
                        pltpu.VMEM((n_streams, MOBA_BLOCK, MOBA_BLOCK), F32),
                        pltpu.VMEM((n_streams, MOBA_VT_ROWS, MOBA_BLOCK), F32)],
        compiler_params=_params("parallel", "parallel", "arbitrary"),
        name="moba_flash",
    )(q_t, k_rot, v_t, k_mean)


def _merge_kernel(a_ref, b_ref, c_ref, gl_ref, x_ref, wb_ref, wo_ref, g_ref, bt_ref,
                  xo_ref, xb_ref):
    gl = gl_ref[...].astype(F32)
    d = D_MODEL
    m = (jax.nn.sigmoid(gl[:, :d]) * _dot(a_ref[...], wb_ref[0])
         + jax.nn.sigmoid(gl[:, d:2 * d]) * _dot(b_ref[...], wb_ref[1])
         + jax.nn.sigmoid(gl[:, 2 * d:]) * _dot(c_ref[...], wb_ref[2]))
    mix = _dot(m.astype(BF16), wo_ref[...])
    y = _layer_norm(DEEPNORM_ALPHA * x_ref[...] + mix, g_ref[...], bt_ref[...])
    xo_ref[...] = y
    xb_ref[...] = y.astype(BF16)


def _merge(a, b, c, proj, x, w_branch, w_out, ln_g, ln_b, layer, *, tm):
    t = x.shape[0]
    br = pl.BlockSpec((tm, BRANCH_W), lambda i: (i, 0))
    xs = pl.BlockSpec((tm, D_MODEL), lambda i: (i, 0))

    def full(shape):
        return pl.BlockSpec((None,) + shape, lambda i: (layer,) + (0,) * len(shape))

    return pl.pallas_call(
        _merge_kernel,
        grid=(t // tm,),
        in_specs=[br, br, br,
                  pl.BlockSpec((tm, 3 * D_MODEL), lambda i: (i, COL_GL // (3 * D_MODEL))),
                  xs, full((3, BRANCH_W, D_MODEL)), full((D_MODEL, D_MODEL)),
                  full((1, D_MODEL)), full((1, D_MODEL))],
        out_specs=[xs, xs],
        out_shape=[jax.ShapeDtypeStruct((t, D_MODEL), F32),
                   jax.ShapeDtypeStruct((t, D_MODEL), BF16)],
        compiler_params=_params("parallel"),
        name="merge_ln1",
    )(a, b, c, proj, x, w_branch, w_out, ln_g, ln_b)


def _ffn_kernel(xb_ref, x_ref, w1_ref, w2_ref, g_ref, bt_ref, xo_ref, xbo_ref):
    h = jnp.square(jnp.maximum(_dot(xb_ref[...], w1_ref[...]), 0.0)).astype(BF16)
    ff = _dot(h, w2_ref[...])
    y = _layer_norm(DEEPNORM_ALPHA * x_ref[...] + ff, g_ref[...], bt_ref[...])
    xo_ref[...] = y
    xbo_ref[...] = y.astype(BF16)


def _ffn(xb, x, w1, w2, ln_g, ln_b, layer, *, tm):
    t = x.shape[0]
    xs = pl.BlockSpec((tm, D_MODEL), lambda i: (i, 0))
    vec = pl.BlockSpec((None, 1, D_MODEL), lambda i: (layer, 0, 0))
    once = pl.Buffered(1)
    return pl.pallas_call(
        _ffn_kernel,
        grid=(t // tm,),
        in_specs=[xs, xs,
                  pl.BlockSpec((None, D_MODEL, D_FF), lambda i: (layer, 0, 0), pipeline_mode=once),
                  pl.BlockSpec((None, D_FF, D_MODEL), lambda i: (layer, 0, 0), pipeline_mode=once),
                  vec, vec],
        out_specs=[xs, xs],
        out_shape=[jax.ShapeDtypeStruct((t, D_MODEL), F32),
                   jax.ShapeDtypeStruct((t, D_MODEL), BF16)],
        compiler_params=_params("parallel"),
        name="ffn_ln2",
    )(xb, x, w1, w2, ln_g, ln_b)


def _split_w_in(w_in):
    sizes = [512, 512, 512, 512, GLA_GATE_RANK, 1024, 512, 512, 512, 3 * D_MODEL]
    offs = np.concatenate([[0], np.cumsum(sizes)])
    gq, gk, gv, gg, glr, gz, mq, mk, mv, gl = [w_in[..., offs[i]:offs[i + 1]] for i in range(10)]
    w_main = jnp.concatenate([gl, gq, gk, gv, gg, gz, mq, mk, mv], axis=-1).astype(BF16)
    w_lr = jnp.pad(glr, ((0, 0), (0, 0), (0, LANES - GLA_GATE_RANK))).astype(BF16)
    return w_main, w_lr


def kernel(x, positions, w_in, w_gate_up, b_gate, gla_norm_w, gmlp_ln_g, gmlp_ln_b, gmlp_w_s,
           gmlp_b_s, w_branch, w_out, ln1_g, ln1_b, w_ff1, w_ff2, ln2_g, ln2_b):
    batch, seq, d = x.shape
    t = batch * seq
    depth = w_in.shape[0]
    assert d == D_MODEL and seq % 512 == 0

    half = ROPE_DIMS // 2
    inv = 1.0 / (ROPE_THETA ** (jnp.arange(half, dtype=F32) * (2.0 / ROPE_DIMS)))
    inv_row = jnp.concatenate([inv, inv, jnp.zeros((LANES - ROPE_DIMS,), F32)])[None, :]
    cos, s1, s2 = [a.reshape(batch, seq, LANES) for a in
                   _rope_tables(positions.reshape(t, 1), inv_row, rows=min(1024, seq))]

    w_main, w_lr = _split_w_in(w_in)
    w_gu = jnp.pad(w_gate_up, ((0, 0), (0, LANES - GLA_GATE_RANK), (0, 0))).astype(BF16)
    bs_full = jnp.repeat(jnp.swapaxes(gmlp_b_s, 1, 2), GMLP_GROUP_CH, axis=2)
    w_branch_b, w_out_b = w_branch.astype(BF16), w_out.astype(BF16)
    w_ff1_b, w_ff2_b = w_ff1.astype(BF16), w_ff2.astype(BF16)
    row = lambda v: v[:, None, :]

    xf = x.reshape(t, d)
    xb = xf.astype(BF16)
    for l in range(depth):
        proj = _matmul(xb, w_main, row(gmlp_ln_g), row(gmlp_ln_b), l,
                       tm=2048, tn=1536, out_dtype=BF16)
        a = _gla(xb, proj, w_lr, w_gu, row(b_gate), row(gla_norm_w), l, batch=batch, rows=1024)
        b = _gmlp(proj, gmlp_w_s, bs_full, l, rows=512)
        q_t, k_rot, v_t, k_mean = _moba_prep(proj.reshape(batch, seq, -1), cos, s1, s2,
                                             rows=min(1024, seq))
        c = _moba_flash(q_t, k_rot, v_t, k_mean.reshape(batch, seq // MOBA_BLOCK, -1),
                        rows=2 if batch % 2 == 0 else 1, heads=MOBA_HEADS, tiles=2)
        xf, xb = _merge(a, b, c.reshape(t, -1), proj, xf, w_branch_b, w_out_b,
                        row(ln1_g), row(ln1_b), l, tm=512)
        xf, xb = _ffn(xb, xf, w_ff1_b, w_ff2_b, row(ln2_g), row(ln2_b), l, tm=512)
    return xf.reshape(batch, seq, d)
```

```python
import functools

import numpy as np
import jax
import jax.numpy as jnp
from jax import lax
from jax.experimental import pallas as pl
from jax.experimental.pallas import tpu as pltpu

D_MODEL = 1024
DEPTH = 4
GLA_HEADS = 4
GLA_DK = 128
GLA_DV = 128
GLA_GATE_RANK = 16
GLA_GATE_NORM = 16.0
GLA_CHUNK = 64
GMLP_GROUPS = 4
GMLP_GROUP_CH = 128
GMLP_WIDTH = 512
GMLP_CHUNK = 128
MOBA_HEADS = 4
MOBA_HD = 128
MOBA_BLOCK = 256
MOBA_TOPK = 3
MOBA_VT_ROWS = MOBA_HD + 16
ROPE_THETA = 500000.0
ROPE_DIMS = MOBA_HD // 4
BRANCH_W = 512
D_FF = 4 * D_MODEL
DEEPNORM_ALPHA = (2 * DEPTH) ** 0.25
LN_EPS = 1e-5
RMS_EPS = 1e-6

LANES = 128
VMEM_LIMIT = 56 * 1024 * 1024

COL_GL = 0
COL_GQ = 3072
COL_GK = 3584
COL_GV = 4096
COL_GG = 4608
COL_GZ = 5120
COL_MQ = 6144
COL_MK = 6656
COL_MV = 7168
PROJ_COLS = 7680

BF16 = jnp.bfloat16
F32 = jnp.float32


def _dot(a, b):
    return jnp.dot(a, b, preferred_element_type=F32)


def _dot_nt(a, b):
    return lax.dot_general(a, b, (((1,), (1,)), ((), ())), preferred_element_type=F32)


def _dot_tn(a, b):
    return lax.dot_general(a, b, (((0,), (0,)), ((), ())), preferred_element_type=F32)


def _params(*sem):
    return pltpu.CompilerParams(dimension_semantics=sem, vmem_limit_bytes=VMEM_LIMIT)


def _layer_norm(y, g, b):
    mu = jnp.mean(y, axis=-1, keepdims=True)
    yc = y - mu
    var = jnp.mean(yc * yc, axis=-1, keepdims=True)
    return yc * lax.rsqrt(var + LN_EPS) * g + b


def _matmul_kernel(x_ref, w_ref, o_ref):
    o_ref[...] = _dot(x_ref[...], w_ref[...]).astype(o_ref.dtype)


def _matmul(x, w, layer, *, tm, tn, out_dtype):
    m, k = x.shape
    n = w.shape[2]
    return pl.pallas_call(
        _matmul_kernel,
        grid=(m // tm, n // tn),
        in_specs=[pl.BlockSpec((tm, k), lambda i, j: (i, 0)),
                  pl.BlockSpec((None, k, tn), lambda i, j: (layer, 0, j))],
        out_specs=pl.BlockSpec((tm, tn), lambda i, j: (i, j)),
        out_shape=jax.ShapeDtypeStruct((m, n), out_dtype),
        compiler_params=_params("parallel", "parallel"),
        name="in_proj",
    )(x, w)


def _gla_kernel(x_ref, q_ref, k_ref, v_ref, g_ref, wlr_ref, wgu_ref, bg_ref, nw_ref,
                o_ref, st_ref, la_ref):
    rows = x_ref.shape[0]
    C = GLA_CHUNK

    @pl.when(pl.program_id(1) == 0)
    def _():
        st_ref[...] = jnp.zeros_like(st_ref)

    lr = _dot(x_ref[...], wlr_ref[...]).astype(BF16)
    z = _dot(lr, wgu_ref[...]) + bg_ref[...]
    la_ref[...] = (jnp.minimum(z, 0.0) - jnp.log1p(jnp.exp(-jnp.abs(z)))) / GLA_GATE_NORM

    row = lax.broadcasted_iota(jnp.int32, (C, C), 0)
    col = lax.broadcasted_iota(jnp.int32, (C, C), 1)
    causal = row >= col
    tril = causal.astype(BF16)
    nw = nw_ref[...]

    nc = rows // C
    hs = [slice(h * LANES, (h + 1) * LANES) for h in range(GLA_HEADS)]
    rs = [slice(c * C, (c + 1) * C) for c in range(nc)]

    q_dec, k_inv, k_end, decay, vv = [], [], [], [], []
    for c in range(nc):
        la = la_ref[rs[c], :]
        hi = la.astype(BF16)
        r1 = la - hi.astype(F32)
        mid = r1.astype(BF16)
        lo = (r1 - mid.astype(F32)).astype(BF16)
        bcum = _dot(tril, hi) + _dot(tril, mid) + _dot(tril, lo)
        b_end = bcum[C - 1:C, :]
        q = q_ref[rs[c], :].astype(F32) * (GLA_DK ** -0.5)
        k = k_ref[rs[c], :].astype(F32)
        q_dec.append((q * jnp.exp(bcum)).astype(BF16))
        k_inv.append((k * jnp.exp(-bcum)).astype(BF16))
        k_end.append((k * jnp.exp(b_end - bcum)).astype(BF16))
        decay.append(jnp.exp(b_end))
        vv.append(v_ref[rs[c], :].astype(BF16))

    attn = [[jnp.where(causal, _dot_nt(q_dec[c][:, sl], k_inv[c][:, sl]), 0.0).astype(BF16)
             for sl in hs] for c in range(nc)]
    kv = [[_dot_tn(vv[c][:, sl], k_end[c][:, sl]) for sl in hs] for c in range(nc)]
    o_intra = [[_dot(attn[c][h], vv[c][:, hs[h]]) for h in range(GLA_HEADS)] for c in range(nc)]

    st = [st_ref[h] for h in range(GLA_HEADS)]
    for c in range(nc):
        g = g_ref[rs[c], :].astype(F32)
        for h, sl in enumerate(hs):
            o = o_intra[c][h] + _dot_nt(q_dec[c][:, sl], st[h].astype(BF16))
            st[h] = decay[c][:, sl] * st[h] + kv[c][h]
            o = o * lax.rsqrt(jnp.mean(o * o, axis=-1, keepdims=True) + RMS_EPS) * nw
            gh = g[:, sl]
            o = o * (gh * jax.nn.sigmoid(gh))
            o_ref[rs[c], sl] = o.astype(o_ref.dtype)
    for h in range(GLA_HEADS):
        st_ref[h] = st[h]


def _gla(xb, proj, w_lr, w_gu, b_gate, norm_w, layer, *, batch, rows):
    t = xb.shape[0]
    steps = t // batch // rows
    w = GLA_HEADS * GLA_DK

    def col(off):
        return pl.BlockSpec((rows, w), lambda b, s: (b * steps + s, off // w))

    def full(shape):
        return pl.BlockSpec((None,) + shape, lambda b, s: (layer,) + (0,) * len(shape))

    return pl.pallas_call(
        _gla_kernel,
        grid=(batch, steps),
        in_specs=[pl.BlockSpec((rows, D_MODEL), lambda b, s: (b * steps + s, 0)),
                  col(COL_GQ), col(COL_GK), col(COL_GV), col(COL_GG),
                  full((D_MODEL, LANES)), full((LANES, w)), full((1, w)), full((1, GLA_DV))],
        out_specs=pl.BlockSpec((rows, w), lambda b, s: (b * steps + s, 0)),
        out_shape=jax.ShapeDtypeStruct((t, w), BF16),
        scratch_shapes=[pltpu.VMEM((GLA_HEADS, GLA_DV, GLA_DK), F32),
                        pltpu.VMEM((rows, w), F32)],
        compiler_params=_params("arbitrary", "arbitrary"),
        name="gla",
    )(xb, proj, proj, proj, proj, w_lr, w_gu, b_gate, norm_w)


def _gmlp_kernel(z_ref, lng_ref, lnb_ref, ws_ref, bs_ref, o_ref):
    rows = z_ref.shape[0]
    C = GMLP_CHUNK
    z = z_ref[...].astype(F32)
    z = 0.5 * z * (1.0 + lax.erf(z * np.float32(np.sqrt(0.5))))
    u = z[:, :GMLP_WIDTH]
    v = _layer_norm(z[:, GMLP_WIDTH:], lng_ref[...], lnb_ref[...]).astype(BF16)
    row = lax.broadcasted_iota(jnp.int32, (C, C), 0)
    col = lax.broadcasted_iota(jnp.int32, (C, C), 1)
    tril = (row >= col).astype(F32)
    for g in range(GMLP_GROUPS):
        sl = slice(g * GMLP_GROUP_CH, (g + 1) * GMLP_GROUP_CH)
        w = (ws_ref[g] * tril).astype(BF16)
        for c in range(rows // C):
            rs = slice(c * C, (c + 1) * C)
            vs = _dot(w, v[rs, sl]) + bs_ref[:, sl]
            o_ref[rs, sl] = (u[rs, sl] * vs).astype(o_ref.dtype)


def _gmlp(proj, ln_g, ln_b, w_s, bs_full, layer, *, rows):
    t = proj.shape[0]

    def full(shape):
        return pl.BlockSpec((None,) + shape, lambda i: (layer,) + (0,) * len(shape))

    return pl.pallas_call(
        _gmlp_kernel,
        grid=(t // rows,),
        in_specs=[pl.BlockSpec((rows, 2 * GMLP_WIDTH), lambda i: (i, COL_GZ // (2 * GMLP_WIDTH))),
                  full((1, GMLP_WIDTH)), full((1, GMLP_WIDTH)),
                  full((GMLP_GROUPS, GMLP_CHUNK, GMLP_CHUNK)), full((GMLP_CHUNK, GMLP_WIDTH))],
        out_specs=pl.BlockSpec((rows, GMLP_WIDTH), lambda i: (i, 0)),
        out_shape=jax.ShapeDtypeStruct((t, GMLP_WIDTH), BF16),
        compiler_params=_params("parallel"),
        name="gmlp",
    )(proj, ln_g, ln_b, w_s, bs_full)


def _rope_table_kernel(pos_ref, inv_ref, c_ref, s1_ref, s2_ref):
    half = ROPE_DIMS // 2
    ang = pos_ref[...].astype(F32) * inv_ref[...]
    lane = lax.broadcasted_iota(jnp.int32, ang.shape, 1)
    cos = jnp.cos(ang)
    sin = jnp.sin(ang)
    c_ref[...] = jnp.where(lane < ROPE_DIMS, cos, 1.0)
    s1_ref[...] = jnp.where(lane < half, -sin, 0.0)
    s2_ref[...] = jnp.where((lane >= half) & (lane < ROPE_DIMS), sin, 0.0)


def _rope_tables(pos_col, inv_row, *, rows):
    t = pos_col.shape[0]
    spec = pl.BlockSpec((rows, LANES), lambda i: (i, 0))
    shp = jax.ShapeDtypeStruct((t, LANES), F32)
    return pl.pallas_call(
        _rope_table_kernel,
        grid=(t // rows,),
        in_specs=[pl.BlockSpec((rows, 1), lambda i: (i, 0)),
                  pl.BlockSpec((1, LANES), lambda i: (0, 0))],
        out_specs=[spec, spec, spec],
        out_shape=[shp, shp, shp],
        compiler_params=_params("parallel"),
        name="rope_tables",
    )(pos_col, inv_row)


def _moba_prep_kernel(q_ref, k_ref, v_ref, c_ref, s1_ref, s2_ref, qt_ref, ko_ref, vt_ref, km_ref):
    half = ROPE_DIMS // 2
    q_scale = np.float32(MOBA_HD ** -0.5 * np.log2(np.e))

    for n in range(q_ref.shape[0] // MOBA_BLOCK):
        rs = slice(n * MOBA_BLOCK, (n + 1) * MOBA_BLOCK)
        cos = c_ref[rs, :]
        s1 = s1_ref[rs, :]
        s2 = s2_ref[rs, :]

        def rope(t):
            return (t * cos + pltpu.roll(t, LANES - half, 1) * s1 + pltpu.roll(t, half, 1) * s2)

        for h in range(MOBA_HEADS):
            sl = slice(h * LANES, (h + 1) * LANES)
            qr = rope(q_ref[rs, sl].astype(F32)) * q_scale
            qt_ref[sl, rs] = qr.T.astype(qt_ref.dtype)
            kr = rope(k_ref[rs, sl].astype(F32))
            ko_ref[rs, sl] = kr.astype(ko_ref.dtype)
            km_ref[n, :, sl] = jnp.mean(kr, axis=0, keepdims=True)
            vt_ref[h * MOBA_VT_ROWS:h * MOBA_VT_ROWS + MOBA_HD, rs] = (
                v_ref[rs, sl].astype(F32).T.astype(vt_ref.dtype))
            vt_ref[h * MOBA_VT_ROWS + MOBA_HD:(h + 1) * MOBA_VT_ROWS, rs] = jnp.ones(
                (MOBA_VT_ROWS - MOBA_HD, MOBA_BLOCK), vt_ref.dtype)


def _moba_prep(proj3, cos3, s13, s23, *, rows):
    batch, seq, _ = proj3.shape
    w = MOBA_HEADS * MOBA_HD
    nb = seq // MOBA_BLOCK
    tab = pl.BlockSpec((None, rows, LANES), lambda b, i: (b, i, 0))

    def col(off):
        return pl.BlockSpec((None, rows, w), lambda b, i: (b, i, off // w))

    vw = MOBA_HEADS * MOBA_VT_ROWS
    return pl.pallas_call(
        _moba_prep_kernel,
        grid=(batch, seq // rows),
        in_specs=[col(COL_MQ), col(COL_MK), col(COL_MV), tab, tab, tab],
        out_specs=[pl.BlockSpec((None, w, rows), lambda b, i: (b, 0, i)),
                   pl.BlockSpec((None, rows, w), lambda b, i: (b, i, 0)),
                   pl.BlockSpec((None, vw, rows), lambda b, i: (b, 0, i)),
                   pl.BlockSpec((None, rows // MOBA_BLOCK, 1, w), lambda b, i: (b, i, 0, 0))],
        out_shape=[jax.ShapeDtypeStruct((batch, w, seq), BF16),
                   jax.ShapeDtypeStruct((batch, seq, w), BF16),
                   jax.ShapeDtypeStruct((batch, vw, seq), BF16),
                   jax.ShapeDtypeStruct((batch, nb, 1, w), F32)],
        compiler_params=_params("parallel", "parallel"),
        name="moba_prep",
    )(proj3, proj3, proj3, cos3, s13, s23)


def _moba_flash_kernel(qt_ref, k_ref, vt_ref, km_ref, o_ref, bias_ref, s_ref, acc_ref, *,
                       heads, tiles):
    BLK = MOBA_BLOCK
    first = pl.program_id(2) * tiles
    n_blk = km_ref.shape[1]
    streams = [(b, h, t) for b in range(qt_ref.shape[0]) for h in range(heads)
               for t in range(tiles)]
    blk = lax.broadcasted_iota(jnp.int32, (n_blk, BLK), 0)
    blk_f = blk.astype(F32)
    key = lax.broadcasted_iota(jnp.int32, (BLK, BLK), 0)
    qry = lax.broadcasted_iota(jnp.int32, (BLK, BLK), 1)
    hs = [slice(h * MOBA_HD, (h + 1) * MOBA_HD) for h in range(heads)]
    vs = [slice(h * MOBA_VT_ROWS, (h + 1) * MOBA_VT_ROWS) for h in range(heads)]
    ts = [slice(t * BLK, (t + 1) * BLK) for t in range(tiles)]

    def scores(b, h, t, start):
        return _dot(k_ref[b, pl.ds(start, BLK), hs[h]], qt_ref[b, hs[h], ts[t]])

    def past_block(n, b, h, m, s_max, j):
        bias = bias_ref[n, pl.ds(j, 1), :]
        m_new = jnp.maximum(m, s_max + bias)
        m_use = jnp.where(m_new == -jnp.inf, 0.0, m_new)
        alpha = jnp.exp2(m - m_use)
        p = jnp.exp2(s_ref[n] - (m_use - bias))
        c0 = pl.multiple_of(j * BLK, BLK)
        acc_ref[n] = alpha * acc_ref[n] + _dot(vt_ref[b, vs[h], pl.ds(c0, BLK)], p.astype(BF16))
        return m_new

    carry0 = []
    for n, (b, h, t) in enumerate(streams):
        own = first + t
        qt = qt_ref[b, hs[h], ts[t]]
        km = km_ref[b, :, hs[h]]
        km_hi = km.astype(BF16)
        km_lo = (km - km_hi.astype(F32)).astype(BF16)
        gate = _dot(km_hi, qt) + _dot(km_lo, qt)
        cand = blk < own
        sel = jnp.zeros(gate.shape, jnp.bool_)
        for r in range(min(MOBA_TOPK, n_blk)):
            g_eff = jnp.where(cand, gate, -jnp.inf)
            best = jnp.max(g_eff, axis=0, keepdims=True)
            idx = jnp.min(jnp.where(cand & (g_eff == best), blk_f, float(n_blk)),
                          axis=0, keepdims=True)
            idx = jnp.where(r < own, idx, float(n_blk))
            sel = sel | (blk_f == idx)
            cand = cand & (blk_f != idx)
        bias_ref[n] = jnp.where(sel, 0.0, -jnp.inf)
        s0 = scores(b, h, t, 0)
        s_ref[n] = s0
        acc_ref[n] = jnp.zeros((MOBA_VT_ROWS, BLK), F32)
        carry0 += [jnp.full((1, BLK), -jnp.inf, F32), jnp.max(s0, axis=0, keepdims=True)]

    def body(j, carry):
        n0 = pl.multiple_of((j + 1) * BLK, BLK)
        out = []
        for n, (b, h, t) in enumerate(streams):
            m, s_max = carry[2 * n:2 * n + 2]
            s_next = scores(b, h, t, n0)
            m_new = past_block(n, b, h, m, s_max, j)
            s_ref[n] = s_next
            out += [m_new, jnp.max(s_next, axis=0, keepdims=True)]
        return tuple(out)

    carry = list(lax.fori_loop(0, first, body, tuple(carry0)))
    for u in range(tiles):
        for n, (b, h, t) in enumerate(streams):
            m, s_max = carry[2 * n:2 * n + 2]
            if t == u:
                s = jnp.where(key <= qry, s_ref[n], -jnp.inf)
                m_new = jnp.maximum(m, jnp.max(s, axis=0, keepdims=True))
                alpha = jnp.exp2(m - m_new)
                p = jnp.exp2(s - m_new)
                r0 = pl.multiple_of((first + u) * BLK, BLK)
                acc = alpha * acc_ref[n] + _dot(vt_ref[b, vs[h], pl.ds(r0, BLK)], p.astype(BF16))
                o_ref[b, ts[t], hs[h]] = (
                    acc[:MOBA_HD] / acc[MOBA_HD:MOBA_HD + 1]).T.astype(o_ref.dtype)
            elif t > u:
                carry[2 * n] = past_block(n, b, h, m, s_max, first + u)
                s_next = scores(b, h, t, pl.multiple_of((first + u + 1) * BLK, BLK))
                s_ref[n] = s_next
                carry[2 * n + 1] = jnp.max(s_next, axis=0, keepdims=True)


def _moba_flash(q_t, k_rot, v_t, k_mean, *, rows, heads, tiles):
    batch, seq, w = k_rot.shape
    n_blk = seq // MOBA_BLOCK
    hw = heads * MOBA_HD
    tq = tiles * MOBA_BLOCK
    n_streams = rows * heads * tiles
    once = pl.Buffered(1)
    return pl.pallas_call(
        functools.partial(_moba_flash_kernel, heads=heads, tiles=tiles),
        grid=(batch // rows, MOBA_HEADS // heads, seq // tq),
        in_specs=[pl.BlockSpec((rows, hw, tq), lambda b, g, i: (b, g, i)),
                  pl.BlockSpec((rows, seq, hw), lambda b, g, i: (b, 0, g), pipeline_mode=once),
                  pl.BlockSpec((rows, heads * MOBA_VT_ROWS, seq), lambda b, g, i: (b, g, 0),
                               pipeline_mode=once),
                  pl.BlockSpec((rows, n_blk, hw), lambda b, g, i: (b, 0, g))],
        out_specs=pl.BlockSpec((rows, tq, hw), lambda b, g, i: (b, i, g)),
        out_shape=jax.ShapeDtypeStruct((batch, seq, w), BF16),
        scratch_shapes=[pltpu.VMEM((n_streams, n_blk, MOBA_BLOCK), F32),
                        pltpu.VMEM((n_streams, MOBA_BLOCK, MOBA_BLOCK), F32),
                        pltpu.VMEM((n_streams, MOBA_VT_ROWS, MOBA_BLOCK), F32)],
        compiler_params=pltpu.CompilerParams(
            dimension_semantics=("parallel", "parallel", "arbitrary"),
            vmem_limit_bytes=62 * 1024 * 1024),
        name="moba_flash",
    )(q_t, k_rot, v_t, k_mean)


def _merge_kernel(a_ref, b_ref, c_ref, gl_ref, x_ref, wb_ref, wo_ref, g_ref, bt_ref,
                  xo_ref, xb_ref):
    gl = gl_ref[...].astype(F32)
    d = D_MODEL
    m = (jax.nn.sigmoid(gl[:, :d]) * _dot(a_ref[...], wb_ref[0])
         + jax.nn.sigmoid(gl[:, d:2 * d]) * _dot(b_ref[...], wb_ref[1])
         + jax.nn.sigmoid(gl[:, 2 * d:]) * _dot(c_ref[...], wb_ref[2]))
    mix = _dot(m.astype(BF16), wo_ref[...])
    y = _layer_norm(DEEPNORM_ALPHA * x_ref[...] + mix, g_ref[...], bt_ref[...])
    xo_ref[...] = y
    xb_ref[...] = y.astype(BF16)


def _merge(a, b, c, proj, x, w_branch, w_out, ln_g, ln_b, layer, *, tm):
    t = x.shape[0]
    br = pl.BlockSpec((tm, BRANCH_W), lambda i: (i, 0))
    xs = pl.BlockSpec((tm, D_MODEL), lambda i: (i, 0))

    def full(shape):
        return pl.BlockSpec((None,) + shape, lambda i: (layer,) + (0,) * len(shape))

    return pl.pallas_call(
        _merge_kernel,
        grid=(t // tm,),
        in_specs=[br, br, br,
                  pl.BlockSpec((tm, 3 * D_MODEL), lambda i: (i, COL_GL // (3 * D_MODEL))),
                  xs, full((3, BRANCH_W, D_MODEL)), full((D_MODEL, D_MODEL)),
                  full((1, D_MODEL)), full((1, D_MODEL))],
        out_specs=[xs, xs],
        out_shape=[jax.ShapeDtypeStruct((t, D_MODEL), F32),
                   jax.ShapeDtypeStruct((t, D_MODEL), BF16)],
        compiler_params=_params("parallel"),
        name="merge_ln1",
    )(a, b, c, proj, x, w_branch, w_out, ln_g, ln_b)


def _ffn_kernel(xb_ref, x_ref, w1_ref, w2_ref, g_ref, bt_ref, xo_ref, xbo_ref):
    h = jnp.square(jnp.maximum(_dot(xb_ref[...], w1_ref[...]), 0.0)).astype(BF16)
    ff = _dot(h, w2_ref[...])
    y = _layer_norm(DEEPNORM_ALPHA * x_ref[...] + ff, g_ref[...], bt_ref[...])
    xo_ref[...] = y
    xbo_ref[...] = y.astype(BF16)


def _ffn(xb, x, w1, w2, ln_g, ln_b, layer, *, tm):
    t = x.shape[0]
    xs = pl.BlockSpec((tm, D_MODEL), lambda i: (i, 0))
    vec = pl.BlockSpec((None, 1, D_MODEL), lambda i: (layer, 0, 0))
    once = pl.Buffered(1)
    return pl.pallas_call(
        _ffn_kernel,
        grid=(t // tm,),
        in_specs=[xs, xs,
                  pl.BlockSpec((None, D_MODEL, D_FF), lambda i: (layer, 0, 0), pipeline_mode=once),
                  pl.BlockSpec((None, D_FF, D_MODEL), lambda i: (layer, 0, 0), pipeline_mode=once),
                  vec, vec],
        out_specs=[xs, xs],
        out_shape=[jax.ShapeDtypeStruct((t, D_MODEL), F32),
                   jax.ShapeDtypeStruct((t, D_MODEL), BF16)],
        compiler_params=_params("parallel"),
        name="ffn_ln2",
    )(xb, x, w1, w2, ln_g, ln_b)


def _split_w_in(w_in):
    sizes = [512, 512, 512, 512, GLA_GATE_RANK, 1024, 512, 512, 512, 3 * D_MODEL]
    offs = np.concatenate([[0], np.cumsum(sizes)])
    gq, gk, gv, gg, glr, gz, mq, mk, mv, gl = [w_in[..., offs[i]:offs[i + 1]] for i in range(10)]
    w_main = jnp.concatenate([gl, gq, gk, gv, gg, gz, mq, mk, mv], axis=-1).astype(BF16)
    w_lr = jnp.pad(glr, ((0, 0), (0, 0), (0, LANES - GLA_GATE_RANK))).astype(BF16)
    return w_main, w_lr


def kernel(x, positions, w_in, w_gate_up, b_gate, gla_norm_w, gmlp_ln_g, gmlp_ln_b, gmlp_w_s,
           gmlp_b_s, w_branch, w_out, ln1_g, ln1_b, w_ff1, w_ff2, ln2_g, ln2_b):
    batch, seq, d = x.shape
    t = batch * seq
    depth = w_in.shape[0]
    assert d == D_MODEL and seq % 512 == 0

    half = ROPE_DIMS // 2
    inv = 1.0 / (ROPE_THETA ** (jnp.arange(half, dtype=F32) * (2.0 / ROPE_DIMS)))
    inv_row = jnp.concatenate([inv, inv, jnp.zeros((LANES - ROPE_DIMS,), F32)])[None, :]
    cos, s1, s2 = [a.reshape(batch, seq, LANES) for a in
                   _rope_tables(positions.reshape(t, 1), inv_row, rows=min(1024, seq))]

    w_main, w_lr = _split_w_in(w_in)
    w_gu = jnp.pad(w_gate_up, ((0, 0), (0, LANES - GLA_GATE_RANK), (0, 0))).astype(BF16)
    bs_full = jnp.repeat(jnp.swapaxes(gmlp_b_s, 1, 2), GMLP_GROUP_CH, axis=2)
    w_branch_b, w_out_b = w_branch.astype(BF16), w_out.astype(BF16)
    w_ff1_b, w_ff2_b = w_ff1.astype(BF16), w_ff2.astype(BF16)
    row = lambda v: v[:, None, :]

    xf = x.reshape(t, d)
    xb = xf.astype(BF16)
    for l in range(depth):
        proj = _matmul(xb, w_main, l, tm=2048, tn=1536, out_dtype=BF16)
        a = _gla(xb, proj, w_lr, w_gu, row(b_gate), row(gla_norm_w), l, batch=batch, rows=1024)
        b = _gmlp(proj, row(gmlp_ln_g), row(gmlp_ln_b), gmlp_w_s, bs_full, l, rows=512)
        q_t, k_rot, v_t, k_mean = _moba_prep(proj.reshape(batch, seq, -1), cos, s1, s2,
                                             rows=min(1024, seq))
        c = _moba_flash(q_t, k_rot, v_t, k_mean.reshape(batch, seq // MOBA_BLOCK, -1),
                        rows=2 if batch % 2 == 0 else 1, heads=MOBA_HEADS, tiles=4)
        xf, xb = _merge(a, b, c.reshape(t, -1), proj, xf, w_branch_b, w_out_b,
                        row(ln1_g), row(ln1_b), l, tm=512)
        xf, xb = _ffn(xb, xf, w_ff1_b, w_ff2_b, row(ln2_g), row(ln2_b), l, tm=512)
    return xf.reshape(batch, seq, d)
```

```python
import functools

import numpy as np
import jax
import jax.numpy as jnp
from jax import lax
from jax.experimental import pallas as pl
from jax.experimental.pallas import tpu as pltpu

D_MODEL = 1024
DEPTH = 4
GLA_HEADS = 4
GLA_DK = 128
GLA_DV = 128
GLA_GATE_RANK = 16
GLA_GATE_NORM = 16.0
GLA_CHUNK = 64
GMLP_GROUPS = 4
GMLP_GROUP_CH = 128
GMLP_WIDTH = 512
GMLP_CHUNK = 128
MOBA_HEADS = 4
MOBA_HD = 128
MOBA_BLOCK = 256
MOBA_TOPK = 3
MOBA_VT_ROWS = MOBA_HD + 16
ROPE_THETA = 500000.0
ROPE_DIMS = MOBA_HD // 4
BRANCH_W = 512
D_FF = 4 * D_MODEL
DEEPNORM_ALPHA = (2 * DEPTH) ** 0.25
LN_EPS = 1e-5
RMS_EPS = 1e-6

LANES = 128
VMEM_LIMIT = 56 * 1024 * 1024

COL_GL = 0
COL_GQ = 3072
COL_GK = 3584
COL_GV = 4096
COL_GG = 4608
COL_GZ = 5120
COL_MQ = 6144
COL_MK = 6656
COL_MV = 7168
PROJ_COLS = 7680

BF16 = jnp.bfloat16
F32 = jnp.float32


def _dot(a, b):
    return jnp.dot(a, b, preferred_element_type=F32)


def _dot_nt(a, b):
    return lax.dot_general(a, b, (((1,), (1,)), ((), ())), preferred_element_type=F32)


def _dot_tn(a, b):
    return lax.dot_general(a, b, (((0,), (0,)), ((), ())), preferred_element_type=F32)


def _params(*sem):
    return pltpu.CompilerParams(dimension_semantics=sem, vmem_limit_bytes=VMEM_LIMIT)


def _layer_norm(y, g, b):
    mu = jnp.mean(y, axis=-1, keepdims=True)
    yc = y - mu
    var = jnp.mean(yc * yc, axis=-1, keepdims=True)
    return yc * lax.rsqrt(var + LN_EPS) * g + b


def _matmul_kernel(x_ref, w_ref, o_ref):
    o_ref[...] = _dot(x_ref[...], w_ref[...]).astype(o_ref.dtype)


def _matmul(x, w, layer, *, tm, tn, out_dtype):
    m, k = x.shape
    n = w.shape[2]
    return pl.pallas_call(
        _matmul_kernel,
        grid=(m // tm, n // tn),
        in_specs=[pl.BlockSpec((tm, k), lambda i, j: (i, 0)),
                  pl.BlockSpec((None, k, tn), lambda i, j: (layer, 0, j))],
        out_specs=pl.BlockSpec((tm, tn), lambda i, j: (i, j)),
        out_shape=jax.ShapeDtypeStruct((m, n), out_dtype),
        compiler_params=_params("parallel", "parallel"),
        name="in_proj",
    )(x, w)


def _gla_kernel(x_ref, q_ref, k_ref, v_ref, g_ref, wlr_ref, wgu_ref, bg_ref, nw_ref,
                o_ref, st_ref, la_ref):
    rows = x_ref.shape[0]
    C = GLA_CHUNK

    @pl.when(pl.program_id(1) == 0)
    def _():
        st_ref[...] = jnp.zeros_like(st_ref)

    lr = _dot(x_ref[...], wlr_ref[...]).astype(BF16)
    z = _dot(lr, wgu_ref[...]) + bg_ref[...]
    la_ref[...] = (jnp.minimum(z, 0.0) - jnp.log1p(jnp.exp(-jnp.abs(z)))) / GLA_GATE_NORM

    row = lax.broadcasted_iota(jnp.int32, (C, C), 0)
    col = lax.broadcasted_iota(jnp.int32, (C, C), 1)
    causal = row >= col
    tril = causal.astype(BF16)
    nw = nw_ref[...]

    nc = rows // C
    hs = [slice(h * LANES, (h + 1) * LANES) for h in range(GLA_HEADS)]
    rs = [slice(c * C, (c + 1) * C) for c in range(nc)]

    q_dec, k_inv, k_end, decay, vv = [], [], [], [], []
    for c in range(nc):
        la = la_ref[rs[c], :]
        hi = la.astype(BF16)
        r1 = la - hi.astype(F32)
        mid = r1.astype(BF16)
        lo = (r1 - mid.astype(F32)).astype(BF16)
        bcum = _dot(tril, hi) + _dot(tril, mid) + _dot(tril, lo)
        b_end = bcum[C - 1:C, :]
        q = q_ref[rs[c], :].astype(F32) * (GLA_DK ** -0.5)
        k = k_ref[rs[c], :].astype(F32)
        q_dec.append((q * jnp.exp(bcum)).astype(BF16))
        k_inv.append((k * jnp.exp(-bcum)).astype(BF16))
        k_end.append((k * jnp.exp(b_end - bcum)).astype(BF16))
        decay.append(jnp.exp(b_end))
        vv.append(v_ref[rs[c], :].astype(BF16))

    attn = [[jnp.where(causal, _dot_nt(q_dec[c][:, sl], k_inv[c][:, sl]), 0.0).astype(BF16)
             for sl in hs] for c in range(nc)]
    kv = [[_dot_tn(vv[c][:, sl], k_end[c][:, sl]) for sl in hs] for c in range(nc)]
    o_intra = [[_dot(attn[c][h], vv[c][:, hs[h]]) for h in range(GLA_HEADS)] for c in range(nc)]

    st = [st_ref[h] for h in range(GLA_HEADS)]
    for c in range(nc):
        g = g_ref[rs[c], :].astype(F32)
        for h, sl in enumerate(hs):
            o = o_intra[c][h] + _dot_nt(q_dec[c][:, sl], st[h].astype(BF16))
            st[h] = decay[c][:, sl] * st[h] + kv[c][h]
            o = o * lax.rsqrt(jnp.mean(o * o, axis=-1, keepdims=True) + RMS_EPS) * nw
            gh = g[:, sl]
            o = o * (gh * jax.nn.sigmoid(gh))
            o_ref[rs[c], sl] = o.astype(o_ref.dtype)
    for h in range(GLA_HEADS):
        st_ref[h] = st[h]


def _gla(xb, proj, w_lr, w_gu, b_gate, norm_w, layer, *, batch, rows):
    t = xb.shape[0]
    steps = t // batch // rows
    w = GLA_HEADS * GLA_DK

    def col(off):
        return pl.BlockSpec((rows, w), lambda b, s: (b * steps + s, off // w))

    def full(shape):
        return pl.BlockSpec((None,) + shape, lambda b, s: (layer,) + (0,) * len(shape))

    return pl.pallas_call(
        _gla_kernel,
        grid=(batch, steps),
        in_specs=[pl.BlockSpec((rows, D_MODEL), lambda b, s: (b * steps + s, 0)),
                  col(COL_GQ), col(COL_GK), col(COL_GV), col(COL_GG),
                  full((D_MODEL, LANES)), full((LANES, w)), full((1, w)), full((1, GLA_DV))],
        out_specs=pl.BlockSpec((rows, w), lambda b, s: (b * steps + s, 0)),
        out_shape=jax.ShapeDtypeStruct((t, w), BF16),
        scratch_shapes=[pltpu.VMEM((GLA_HEADS, GLA_DV, GLA_DK), F32),
                        pltpu.VMEM((rows, w), F32)],
        compiler_params=_params("arbitrary", "arbitrary"),
        name="gla",
    )(xb, proj, proj, proj, proj, w_lr, w_gu, b_gate, norm_w)


def _gmlp_kernel(z_ref, lng_ref, lnb_ref, ws_ref, bs_ref, o_ref):
    rows = z_ref.shape[0]
    C = GMLP_CHUNK
    z = z_ref[...].astype(F32)
    z = 0.5 * z * (1.0 + lax.erf(z * np.float32(np.sqrt(0.5))))
    u = z[:, :GMLP_WIDTH]
    v = _layer_norm(z[:, GMLP_WIDTH:], lng_ref[...], lnb_ref[...]).astype(BF16)
    row = lax.broadcasted_iota(jnp.int32, (C, C), 0)
    col = lax.broadcasted_iota(jnp.int32, (C, C), 1)
    tril = (row >= col).astype(F32)
    for g in range(GMLP_GROUPS):
        sl = slice(g * GMLP_GROUP_CH, (g + 1) * GMLP_GROUP_CH)
        w = (ws_ref[g] * tril).astype(BF16)
        for c in range(rows // C):
            rs = slice(c * C, (c + 1) * C)
            vs = _dot(w, v[rs, sl]) + bs_ref[:, sl]
            o_ref[rs, sl] = (u[rs, sl] * vs).astype(o_ref.dtype)


def _gmlp(proj, ln_g, ln_b, w_s, bs_full, layer, *, rows):
    t = proj.shape[0]

    def full(shape):
        return pl.BlockSpec((None,) + shape, lambda i: (layer,) + (0,) * len(shape))

    return pl.pallas_call(
        _gmlp_kernel,
        grid=(t // rows,),
        in_specs=[pl.BlockSpec((rows, 2 * GMLP_WIDTH), lambda i: (i, COL_GZ // (2 * GMLP_WIDTH))),
                  full((1, GMLP_WIDTH)), full((1, GMLP_WIDTH)),
                  full((GMLP_GROUPS, GMLP_CHUNK, GMLP_CHUNK)), full((GMLP_CHUNK, GMLP_WIDTH))],
        out_specs=pl.BlockSpec((rows, GMLP_WIDTH), lambda i: (i, 0)),
        out_shape=jax.ShapeDtypeStruct((t, GMLP_WIDTH), BF16),
        compiler_params=_params("parallel"),
        name="gmlp",
    )(proj, ln_g, ln_b, w_s, bs_full)


def _rope_table_kernel(pos_ref, inv_ref, c_ref, s1_ref, s2_ref):
    half = ROPE_DIMS // 2
    ang = pos_ref[...].astype(F32) * inv_ref[...]
    lane = lax.broadcasted_iota(jnp.int32, ang.shape, 1)
    cos = jnp.cos(ang)
    sin = jnp.sin(ang)
    c_ref[...] = jnp.where(lane < ROPE_DIMS, cos, 1.0)
    s1_ref[...] = jnp.where(lane < half, -sin, 0.0)
    s2_ref[...] = jnp.where((lane >= half) & (lane < ROPE_DIMS), sin, 0.0)


def _rope_tables(pos_col, inv_row, *, rows):
    t = pos_col.shape[0]
    spec = pl.BlockSpec((rows, LANES), lambda i: (i, 0))
    shp = jax.ShapeDtypeStruct((t, LANES), F32)
    return pl.pallas_call(
        _rope_table_kernel,
        grid=(t // rows,),
        in_specs=[pl.BlockSpec((rows, 1), lambda i: (i, 0)),
                  pl.BlockSpec((1, LANES), lambda i: (0, 0))],
        out_specs=[spec, spec, spec],
        out_shape=[shp, shp, shp],
        compiler_params=_params("parallel"),
        name="rope_tables",
    )(pos_col, inv_row)


def _moba_prep_kernel(q_ref, k_ref, v_ref, c_ref, s1_ref, s2_ref, qt_ref, ko_ref, vt_ref, km_ref):
    half = ROPE_DIMS // 2
    q_scale = np.float32(MOBA_HD ** -0.5 * np.log2(np.e))

    for n in range(q_ref.shape[0] // MOBA_BLOCK):
        rs = slice(n * MOBA_BLOCK, (n + 1) * MOBA_BLOCK)
        cos = c_ref[rs, :]
        s1 = s1_ref[rs, :]
        s2 = s2_ref[rs, :]

        def rope(t):
            return (t * cos + pltpu.roll(t, LANES - half, 1) * s1 + pltpu.roll(t, half, 1) * s2)

        for h in range(MOBA_HEADS):
            sl = slice(h * LANES, (h + 1) * LANES)
            qr = rope(q_ref[rs, sl].astype(F32)) * q_scale
            qt_ref[sl, rs] = qr.T.astype(qt_ref.dtype)
            kr = rope(k_ref[rs, sl].astype(F32))
            ko_ref[rs, sl] = kr.astype(ko_ref.dtype)
            km_ref[n, :, sl] = jnp.mean(kr, axis=0, keepdims=True)
            vt_ref[h * MOBA_VT_ROWS:h * MOBA_VT_ROWS + MOBA_HD, rs] = (
                v_ref[rs, sl].astype(F32).T.astype(vt_ref.dtype))
            vt_ref[h * MOBA_VT_ROWS + MOBA_HD:(h + 1) * MOBA_VT_ROWS, rs] = jnp.ones(
                (MOBA_VT_ROWS - MOBA_HD, MOBA_BLOCK), vt_ref.dtype)


def _moba_prep(proj3, cos3, s13, s23, *, rows):
    batch, seq, _ = proj3.shape
    w = MOBA_HEADS * MOBA_HD
    nb = seq // MOBA_BLOCK
    tab = pl.BlockSpec((None, rows, LANES), lambda b, i: (b, i, 0))

    def col(off):
        return pl.BlockSpec((None, rows, w), lambda b, i: (b, i, off // w))

    vw = MOBA_HEADS * MOBA_VT_ROWS
    return pl.pallas_call(
        _moba_prep_kernel,
        grid=(batch, seq // rows),
        in_specs=[col(COL_MQ), col(COL_MK), col(COL_MV), tab, tab, tab],
        out_specs=[pl.BlockSpec((None, w, rows), lambda b, i: (b, 0, i)),
                   pl.BlockSpec((None, rows, w), lambda b, i: (b, i, 0)),
                   pl.BlockSpec((None, vw, rows), lambda b, i: (b, 0, i)),
                   pl.BlockSpec((None, rows // MOBA_BLOCK, 1, w), lambda b, i: (b, i, 0, 0))],
        out_shape=[jax.ShapeDtypeStruct((batch, w, seq), BF16),
                   jax.ShapeDtypeStruct((batch, seq, w), BF16),
                   jax.ShapeDtypeStruct((batch, vw, seq), BF16),
                   jax.ShapeDtypeStruct((batch, nb, 1, w), F32)],
        compiler_params=_params("parallel", "parallel"),
        name="moba_prep",
    )(proj3, proj3, proj3, cos3, s13, s23)


def _moba_flash_kernel(qt_ref, k_ref, vt_ref, km_ref, o_ref, bias_ref, s_ref, acc_ref, *,
                       heads, tiles):
    BLK = MOBA_BLOCK
    first = pl.program_id(2) * tiles
    n_blk = km_ref.shape[1]
    streams = [(b, h, t) for b in range(qt_ref.shape[0]) for h in range(heads)
               for t in range(tiles)]
    blk = lax.broadcasted_iota(jnp.int32, (n_blk, BLK), 0)
    blk_f = blk.astype(F32)
    key = lax.broadcasted_iota(jnp.int32, (BLK, BLK), 0)
    qry = lax.broadcasted_iota(jnp.int32, (BLK, BLK), 1)
    hs = [slice(h * MOBA_HD, (h + 1) * MOBA_HD) for h in range(heads)]
    vs = [slice(h * MOBA_VT_ROWS, (h + 1) * MOBA_VT_ROWS) for h in range(heads)]
    ts = [slice(t * BLK, (t + 1) * BLK) for t in range(tiles)]

    def scores(b, h, t, start):
        return _dot(k_ref[b, pl.ds(start, BLK), hs[h]], qt_ref[b, hs[h], ts[t]])

    def past_block(n, b, h, m, s_max, j):
        bias = bias_ref[n, pl.ds(j, 1), :]
        m_new = jnp.maximum(m, s_max + bias)
        m_use = jnp.where(m_new == -jnp.inf, 0.0, m_new)
        alpha = jnp.exp2(m - m_use)
        p = jnp.exp2(s_ref[n] - (m_use - bias))
        c0 = pl.multiple_of(j * BLK, BLK)
        acc_ref[n] = alpha * acc_ref[n] + _dot(vt_ref[b, vs[h], pl.ds(c0, BLK)], p.astype(BF16))
        return m_new

    carry0 = []
    for n, (b, h, t) in enumerate(streams):
        own = first + t
        qt = qt_ref[b, hs[h], ts[t]]
        km = km_ref[b, :, hs[h]]
        km_hi = km.astype(BF16)
        km_lo = (km - km_hi.astype(F32)).astype(BF16)
        gate = _dot(km_hi, qt) + _dot(km_lo, qt)
        cand = blk < own
        sel = jnp.zeros(gate.shape, jnp.bool_)
        for r in range(min(MOBA_TOPK, n_blk)):
            g_eff = jnp.where(cand, gate, -jnp.inf)
            best = jnp.max(g_eff, axis=0, keepdims=True)
            idx = jnp.min(jnp.where(cand & (g_eff == best), blk_f, float(n_blk)),
                          axis=0, keepdims=True)
            idx = jnp.where(r < own, idx, float(n_blk))
            sel = sel | (blk_f == idx)
            cand = cand & (blk_f != idx)
        bias_ref[n] = jnp.where(sel, 0.0, -jnp.inf)
        s0 = scores(b, h, t, 0)
        s_ref[n] = s0
        acc_ref[n] = jnp.zeros((MOBA_VT_ROWS, BLK), F32)
        carry0 += [jnp.full((1, BLK), -jnp.inf, F32), jnp.max(s0, axis=0, keepdims=True)]

    def body(j, carry):
        n0 = pl.multiple_of((j + 1) * BLK, BLK)
        out = []
        for n, (b, h, t) in enumerate(streams):
            m, s_max = carry[2 * n:2 * n + 2]
            s_next = scores(b, h, t, n0)
            m_new = past_block(n, b, h, m, s_max, j)
            s_ref[n] = s_next
            out += [m_new, jnp.max(s_next, axis=0, keepdims=True)]
        return tuple(out)

    carry = list(lax.fori_loop(0, first, body, tuple(carry0)))
    for u in range(tiles):
        for n, (b, h, t) in enumerate(streams):
            m, s_max = carry[2 * n:2 * n + 2]
            if t == u:
                s = jnp.where(key <= qry, s_ref[n], -jnp.inf)
                m_new = jnp.maximum(m, jnp.max(s, axis=0, keepdims=True))
                alpha = jnp.exp2(m - m_new)
                p = jnp.exp2(s - m_new)
                r0 = pl.multiple_of((first + u) * BLK, BLK)
                acc = alpha * acc_ref[n] + _dot(vt_ref[b, vs[h], pl.ds(r0, BLK)], p.astype(BF16))
                o_ref[b, ts[t], hs[h]] = (
                    acc[:MOBA_HD] / acc[MOBA_HD:MOBA_HD + 1]).T.astype(o_ref.dtype)
            elif t > u:
                carry[2 * n] = past_block(n, b, h, m, s_max, first + u)
                s_next = scores(b, h, t, pl.multiple_of((first + u + 1) * BLK, BLK))
                s_ref[n] = s_next
                carry[2 * n + 1] = jnp.max(s_next, axis=0, keepdims=True)


def _moba_flash(q_t, k_rot, v_t, k_mean, *, rows, heads, tiles):
    batch, seq, w = k_rot.shape
    n_blk = seq // MOBA_BLOCK
    hw = heads * MOBA_HD
    tq = tiles * MOBA_BLOCK
    n_streams = rows * heads * tiles
    once = pl.Buffered(1)
    return pl.pallas_call(
        functools.partial(_moba_flash_kernel, heads=heads, tiles=tiles),
        grid=(batch // rows, MOBA_HEADS // heads, seq // tq),
        in_specs=[pl.BlockSpec((rows, hw, tq), lambda b, g, i: (b, g, i)),
                  pl.BlockSpec((rows, seq, hw), lambda b, g, i: (b, 0, g), pipeline_mode=once),
                  pl.BlockSpec((rows, heads * MOBA_VT_ROWS, seq), lambda b, g, i: (b, g, 0),
                               pipeline_mode=once),
                  pl.BlockSpec((rows, n_blk, hw), lambda b, g, i: (b, 0, g))],
        out_specs=pl.BlockSpec((rows, tq, hw), lambda b, g, i: (b, i, g)),
        out_shape=jax.ShapeDtypeStruct((batch, seq, w), BF16),
        scratch_shapes=[pltpu.VMEM((n_streams, n_blk, MOBA_BLOCK), F32),
                        pltpu.VMEM((n_streams, MOBA_BLOCK, MOBA_BLOCK), F32),
                        pltpu.VMEM((n_streams, MOBA_VT_ROWS, MOBA_BLOCK), F32)],
        compiler_params=pltpu.CompilerParams(
            dimension_semantics=("parallel", "parallel", "arbitrary"),
            vmem_limit_bytes=62 * 1024 * 1024),
        name="moba_flash",
    )(q_t, k_rot, v_t, k_mean)


def _merge_kernel(a_ref, b_ref, c_ref, gl_ref, x_ref, wb_ref, wo_ref, g_ref, bt_ref,
                  xo_ref, xb_ref):
    gl = gl_ref[...].astype(F32)
    d = D_MODEL
    m = (jax.nn.sigmoid(gl[:, :d]) * _dot(a_ref[...], wb_ref[0])
         + jax.nn.sigmoid(gl[:, d:2 * d]) * _dot(b_ref[...], wb_ref[1])
         + jax.nn.sigmoid(gl[:, 2 * d:]) * _dot(c_ref[...], wb_ref[2]))
    mix = _dot(m.astype(BF16), wo_ref[...])
    y = _layer_norm(DEEPNORM_ALPHA * x_ref[...] + mix, g_ref[...], bt_ref[...])
    xo_ref[...] = y
    xb_ref[...] = y.astype(BF16)


def _merge(a, b, c, proj, x, w_branch, w_out, ln_g, ln_b, layer, *, tm):
    t = x.shape[0]
    br = pl.BlockSpec((tm, BRANCH_W), lambda i: (i, 0))
    xs = pl.BlockSpec((tm, D_MODEL), lambda i: (i, 0))

    def full(shape):
        return pl.BlockSpec((None,) + shape, lambda i: (layer,) + (0,) * len(shape))

    return pl.pallas_call(
        _merge_kernel,
        grid=(t // tm,),
        in_specs=[br, br, br,
                  pl.BlockSpec((tm, 3 * D_MODEL), lambda i: (i, COL_GL // (3 * D_MODEL))),
                  xs, full((3, BRANCH_W, D_MODEL)), full((D_MODEL, D_MODEL)),
                  full((1, D_MODEL)), full((1, D_MODEL))],
        out_specs=[xs, xs],
        out_shape=[jax.ShapeDtypeStruct((t, D_MODEL), F32),
                   jax.ShapeDtypeStruct((t, D_MODEL), BF16)],
        compiler_params=_params("parallel"),
        name="merge_ln1",
    )(a, b, c, proj, x, w_branch, w_out, ln_g, ln_b)


def _ffn_kernel(xb_ref, x_ref, w1_ref, w2_ref, g_ref, bt_ref, xo_ref, xbo_ref):
    h = jnp.square(jnp.maximum(_dot(xb_ref[...], w1_ref[...]), 0.0)).astype(BF16)
    ff = _dot(h, w2_ref[...])
    y = _layer_norm(DEEPNORM_ALPHA * x_ref[...] + ff, g_ref[...], bt_ref[...])
    xo_ref[...] = y
    xbo_ref[...] = y.astype(BF16)


def _ffn(xb, x, w1, w2, ln_g, ln_b, layer, *, tm):
    t = x.shape[0]
    xs = pl.BlockSpec((tm, D_MODEL), lambda i: (i, 0))
    vec = pl.BlockSpec((None, 1, D_MODEL), lambda i: (layer, 0, 0))
    once = pl.Buffered(1)
    return pl.pallas_call(
        _ffn_kernel,
        grid=(t // tm,),
        in_specs=[xs, xs,
                  pl.BlockSpec((None, D_MODEL, D_FF), lambda i: (layer, 0, 0), pipeline_mode=once),
                  pl.BlockSpec((None, D_FF, D_MODEL), lambda i: (layer, 0, 0), pipeline_mode=once),
                  vec, vec],
        out_specs=[xs, xs],
        out_shape=[jax.ShapeDtypeStruct((t, D_MODEL), F32),
                   jax.ShapeDtypeStruct((t, D_MODEL), BF16)],
        compiler_params=_params("parallel"),
        name="ffn_ln2",
    )(xb, x, w1, w2, ln_g, ln_b)


def _split_w_in(w_in):
    sizes = [512, 512, 512, 512, GLA_GATE_RANK, 1024, 512, 512, 512, 3 * D_MODEL]
    offs = np.concatenate([[0], np.cumsum(sizes)])
    gq, gk, gv, gg, glr, gz, mq, mk, mv, gl = [w_in[..., offs[i]:offs[i + 1]] for i in range(10)]
    w_main = jnp.concatenate([gl, gq, gk, gv, gg, gz, mq, mk, mv], axis=-1).astype(BF16)
    w_lr = jnp.pad(glr, ((0, 0), (0, 0), (0, LANES - GLA_GATE_RANK))).astype(BF16)
    return w_main, w_lr


def kernel(x, positions, w_in, w_gate_up, b_gate, gla_norm_w, gmlp_ln_g, gmlp_ln_b, gmlp_w_s,
           gmlp_b_s, w_branch, w_out, ln1_g, ln1_b, w_ff1, w_ff2, ln2_g, ln2_b):
    batch, seq, d = x.shape
    t = batch * seq
    depth = w_in.shape[0]
    assert d == D_MODEL and seq % 512 == 0

    half = ROPE_DIMS // 2
    inv = 1.0 / (ROPE_THETA ** (jnp.arange(half, dtype=F32) * (2.0 / ROPE_DIMS)))
    inv_row = jnp.concatenate([inv, inv, jnp.zeros((LANES - ROPE_DIMS,), F32)])[None, :]
    cos, s1, s2 = [a.reshape(batch, seq, LANES) for a in
                   _rope_tables(positions.reshape(t, 1), inv_row, rows=min(1024, seq))]

    w_main, w_lr = _split_w_in(w_in)
    w_gu = jnp.pad(w_gate_up, ((0, 0), (0, LANES - GLA_GATE_RANK), (0, 0))).astype(BF16)
    bs_full = jnp.repeat(jnp.swapaxes(gmlp_b_s, 1, 2), GMLP_GROUP_CH, axis=2)
    w_branch_b, w_out_b = w_branch.astype(BF16), w_out.astype(BF16)
    w_ff1_b, w_ff2_b = w_ff1.astype(BF16), w_ff2.astype(BF16)
    row = lambda v: v[:, None, :]

    xf = x.reshape(t, d)
    xb = xf.astype(BF16)
    for l in range(depth):
        proj = _matmul(xb, w_main, l, tm=2048, tn=1536, out_dtype=BF16)
        a = _gla(xb, proj, w_lr, w_gu, row(b_gate), row(gla_norm_w), l, batch=batch, rows=1024)
        b = _gmlp(proj, row(gmlp_ln_g), row(gmlp_ln_b), gmlp_w_s, bs_full, l, rows=1024)
        q_t, k_rot, v_t, k_mean = _moba_prep(proj.reshape(batch, seq, -1), cos, s1, s2,
                                             rows=min(2048, seq))
        c = _moba_flash(q_t, k_rot, v_t, k_mean.reshape(batch, seq // MOBA_BLOCK, -1),
                        rows=2 if batch % 2 == 0 else 1, heads=MOBA_HEADS, tiles=4)
        xf, xb = _merge(a, b, c.reshape(t, -1), proj, xf, w_branch_b, w_out_b,
                        row(ln1_g), row(ln1_b), l, tm=512)
        xf, xb = _ffn(xb, xf, w_ff1_b, w_ff2_b, row(ln2_g), row(ln2_b), l, tm=512)
    return xf.reshape(batch, seq, d)
```
